```python
import jax, jax.numpy as jnp
from jax import lax
import numpy as np

D_MODEL = 1024
BATCH = 8
SEQ = 4096
DEPTH = 2
DEC_BATCH = 32
DEC_SEQ = 8
PAST_LEN = 16384
PAGE_SIZE = 128

SB_HEADS = 16
SB_HEAD_DIM = D_MODEL // SB_HEADS
SB_WIDTH = SB_HEADS * SB_HEAD_DIM
Q_BLOCK = 128
SB_BIAS_INIT = -6.0
GLA_HEADS = 4
GLA_KEY_WIDTH = D_MODEL // 2
GLA_VAL_WIDTH = D_MODEL
GLA_DK = GLA_KEY_WIDTH // GLA_HEADS
GLA_DV = GLA_VAL_WIDTH // GLA_HEADS
GLA_GATE_RANK = 16
GLA_GATE_TAU = 16.0
GLA_CHUNK = 64
D_FF = 4 * D_MODEL
RMS_EPS = 1e-6
POOL_NUM = 5
POOL_DEN = 4
IN_SPLITS = (SB_WIDTH, SB_WIDTH, SB_WIDTH, GLA_KEY_WIDTH, GLA_KEY_WIDTH, GLA_VAL_WIDTH, GLA_VAL_WIDTH, GLA_GATE_RANK, D_MODEL, D_MODEL)
IN_WIDTH = sum(IN_SPLITS)

kernel_name = 'sb_gla_hybrid_decode_step'


def rms_norm(x, g):
    xf = x.astype(jnp.float32)
    y = xf * lax.rsqrt(jnp.mean(xf * xf, axis=-1, keepdims=True) + RMS_EPS)
    return (y * g.astype(jnp.float32)).astype(x.dtype)


def sb_weights(z, mask):
    ls_neg = jnp.where(mask, jax.nn.log_sigmoid(-z), 0.0)
    rev_excl = lax.cumsum(ls_neg, axis=z.ndim - 1, reverse=True) - ls_neg
    return jnp.where(mask, jnp.exp(jax.nn.log_sigmoid(z) + rev_excl), 0.0)


def sb_prompt(q, k, v, bias):
    B, T, H, d = q.shape
    scale = d ** -0.5
    kpos = jnp.arange(T)
    hb = bias.astype(jnp.float32)[None, :, None, None]

    def one_block(i):
        start = i * Q_BLOCK
        qb = lax.dynamic_slice_in_dim(q, start, Q_BLOCK, axis=1)
        qpos = start + jnp.arange(Q_BLOCK)
        z = jnp.einsum('bqhd,bkhd->bhqk', qb, k, preferred_element_type=jnp.float32) * scale + hb
        w = sb_weights(z, kpos[None, :] < qpos[:, None])
        return jnp.einsum('bhqk,bkhd->bqhd', w.astype(v.dtype), v)

    out = lax.map(one_block, jnp.arange(T // Q_BLOCK))
    return jnp.moveaxis(out, 0, 1).reshape(B, T, H, d)


def sb_paged(q, k_new, v_new, bias, k_past, v_past):
    T = q.shape[1]
    P = k_past.shape[1]
    scale = q.shape[-1] ** -0.5
    z_past = jnp.einsum('bqhd,bkhd->bhqk', q, k_past, preferred_element_type=jnp.float32)
    z_new = jnp.einsum('bqhd,bkhd->bhqk', q, k_new, preferred_element_type=jnp.float32)
    z = jnp.concatenate([z_past, z_new], axis=-1) * scale + bias.astype(jnp.float32)[None, :, None, None]
    tq = jnp.arange(T)
    mask = jnp.concatenate([jnp.ones((T, P), dtype=bool), tq[None, :] < tq[:, None]], axis=-1)
    w = sb_weights(z, mask).astype(v_new.dtype)
    return (jnp.einsum('bhqk,bkhd->bqhd', w[..., :P], v_past)
            + jnp.einsum('bhqk,bkhd->bqhd', w[..., P:], v_new))


def gla_recurrent(q, k, v, log_a, S0):
    B, T, H, dk = q.shape
    dv = v.shape[-1]
    C = GLA_CHUNK if T % GLA_CHUNK == 0 else T
    n = T // C

    def chunks(a):
        return jnp.moveaxis(a.astype(jnp.float32).reshape(B, n, C, H, a.shape[-1]), 1, 0)

    causal = jnp.tril(jnp.ones((C, C), dtype=bool))[None, :, :, None, None]

    def step(S, inp):
        qc, kc, vc, ac = inp
        b = jnp.cumsum(ac, axis=1)
        o_inter = jnp.einsum('bthk,bhkv->bthv', qc * jnp.exp(b), S)
        decay = jnp.exp(jnp.where(causal, b[:, :, None] - b[:, None, :], -jnp.inf))
        scores = jnp.einsum('bthk,bshk,btshk->bhts', qc, kc, decay)
        o_intra = jnp.einsum('bhts,bshv->bthv', scores, vc)
        b_last = b[:, -1]
        S_new = (S * jnp.exp(b_last)[..., None]
                 + jnp.einsum('bshk,bshv->bhkv', kc * jnp.exp(b_last[:, None] - b), vc))
        return S_new, o_inter + o_intra

    S_fin, o = lax.scan(step, S0.astype(jnp.float32), (chunks(q), chunks(k), chunks(v), chunks(log_a)))
    return jnp.moveaxis(o, 0, 1).reshape(B, T, H, dv), S_fin


def hybrid_layer(x, sb_core, S0, norm1_g, w_in, q_norm_g, k_norm_g, sb_bias, w_alpha2, b_alpha,
                 gla_norm_g, b_merge, w_out, norm2_g, w_up, w_down):
    B, T, _ = x.shape
    xn = rms_norm(x, norm1_g)
    proj = xn @ w_in
    qa, ka, va, qb, kb, vb, rb, alr, ga, gb = jnp.split(proj, np.cumsum(IN_SPLITS)[:-1].tolist(), axis=-1)
    qa = rms_norm(qa.reshape(B, T, SB_HEADS, SB_HEAD_DIM), q_norm_g)
    ka = rms_norm(ka.reshape(B, T, SB_HEADS, SB_HEAD_DIM), k_norm_g)
    va = va.reshape(B, T, SB_HEADS, SB_HEAD_DIM)
    o_a = sb_core(qa, ka, va, sb_bias).reshape(B, T, SB_WIDTH)
    log_a = jax.nn.log_sigmoid((alr @ w_alpha2 + b_alpha).astype(jnp.float32)) / GLA_GATE_TAU
    o_b, S_new = gla_recurrent(
        qb.reshape(B, T, GLA_HEADS, GLA_DK) * (GLA_DK ** -0.5),
        kb.reshape(B, T, GLA_HEADS, GLA_DK),
        vb.reshape(B, T, GLA_HEADS, GLA_DV),
        log_a.reshape(B, T, GLA_HEADS, GLA_DK), S0)
    o_b = rms_norm(o_b.astype(x.dtype), gla_norm_g).reshape(B, T, GLA_VAL_WIDTH) * jax.nn.silu(rb)
    mixed = jax.nn.sigmoid(ga + b_merge[0]) * o_a + jax.nn.sigmoid(gb + b_merge[1]) * o_b
    h = x + mixed @ w_out
    u = jax.nn.relu(rms_norm(h, norm2_g) @ w_up)
    y = h + (u * u) @ w_down
    return y, ka, va, S_new


def setup_inputs(seed: int = 0) -> dict:
    key = jax.random.key(seed)
    ks = jax.random.split(key, 20)
    n_pages = PAST_LEN // PAGE_SIZE
    n_used = DEC_BATCH * n_pages
    n_pool = (n_used * POOL_NUM) // POOL_DEN
    f32 = jnp.float32
    nrm = lambda k, shape, s=1.0: jax.random.normal(k, shape, dtype=f32) * s
    page_table = jax.random.permutation(ks[5], n_pool)[:n_used].reshape(DEC_BATCH, n_pages).astype(jnp.int32)
    return {
        'x_prompt': nrm(ks[0], (BATCH, SEQ, D_MODEL)),
        'x_sample': nrm(ks[1], (DEC_BATCH, DEC_SEQ, D_MODEL)),
        'cache_k': nrm(ks[2], (DEPTH, n_pool, PAGE_SIZE, SB_HEADS, SB_HEAD_DIM)),
        'cache_v': nrm(ks[3], (DEPTH, n_pool, PAGE_SIZE, SB_HEADS, SB_HEAD_DIM)),
        'state_gla': nrm(ks[4], (DEPTH, DEC_BATCH, GLA_HEADS, GLA_DK, GLA_DV), 0.5),
        'page_table': page_table,
        'norm1_g': 1.0 + nrm(ks[6], (DEPTH, D_MODEL), 0.02),
        'w_in': nrm(ks[7], (DEPTH, D_MODEL, IN_WIDTH), D_MODEL ** -0.5),
        'q_norm_g': 1.0 + nrm(ks[8], (DEPTH, SB_HEAD_DIM), 0.02),
        'k_norm_g': 1.0 + nrm(ks[9], (DEPTH, SB_HEAD_DIM), 0.02),
        'sb_bias': SB_BIAS_INIT + nrm(ks[18], (DEPTH, SB_HEADS), 0.1),
        'w_alpha2': nrm(ks[10], (DEPTH, GLA_GATE_RANK, GLA_KEY_WIDTH), GLA_GATE_RANK ** -0.5),
        'b_alpha': nrm(ks[11], (DEPTH, GLA_KEY_WIDTH), 0.1),
        'gla_norm_g': 1.0 + nrm(ks[12], (DEPTH, GLA_DV), 0.02),
        'b_merge': nrm(ks[13], (DEPTH, 2, D_MODEL), 0.1),
        'w_out': nrm(ks[14], (DEPTH, D_MODEL, D_MODEL), D_MODEL ** -0.5),
        'norm2_g': 1.0 + nrm(ks[15], (DEPTH, D_MODEL), 0.02),
        'w_up': nrm(ks[16], (DEPTH, D_MODEL, D_FF), D_MODEL ** -0.5),
        'w_down': nrm(ks[17], (DEPTH, D_FF, D_MODEL), D_FF ** -0.5),
    }


def reference(x_prompt, x_sample, cache_k, cache_v, state_gla, page_table, norm1_g, w_in,
              q_norm_g, k_norm_g, sb_bias, w_alpha2, b_alpha, gla_norm_g, b_merge, w_out, norm2_g,
              w_up, w_down):
    yp, ys = x_prompt, x_sample
    bp, bs = x_prompt.shape[0], x_sample.shape[0]
    kp_l, vp_l, sp_l, ks_l, vs_l, ss_l = [], [], [], [], [], []
    for l in range(DEPTH):
        params = (norm1_g[l], w_in[l], q_norm_g[l], k_norm_g[l], sb_bias[l], w_alpha2[l], b_alpha[l],
                  gla_norm_g[l], b_merge[l], w_out[l], norm2_g[l], w_up[l], w_down[l])
        s0 = jnp.zeros((bp, GLA_HEADS, GLA_DK, GLA_DV), dtype=jnp.float32)
        yp, kp, vp, sp = hybrid_layer(yp, sb_prompt, s0, *params)
        k_past = cache_k[l][page_table].reshape(bs, -1, SB_HEADS, SB_HEAD_DIM)
        v_past = cache_v[l][page_table].reshape(bs, -1, SB_HEADS, SB_HEAD_DIM)
        core = lambda q, k, v, b, kp_=k_past, vp_=v_past: sb_paged(q, k, v, b, kp_, vp_)
        ys, ksm, vsm, ssm = hybrid_layer(ys, core, state_gla[l], *params)
        kp_l.append(kp); vp_l.append(vp); sp_l.append(sp)
        ks_l.append(ksm); vs_l.append(vsm); ss_l.append(ssm)
    k_prompt = jnp.stack(kp_l)
    v_prompt = jnp.stack(vp_l)
    gla_prompt = jnp.stack(sp_l).astype(state_gla.dtype)
    k_sample = jnp.stack(ks_l)
    v_sample = jnp.stack(vs_l)
    gla_sample = jnp.stack(ss_l).astype(state_gla.dtype)
    return (yp, ys, k_prompt, v_prompt, gla_prompt, k_sample, v_sample, gla_sample)
```

```python
import functools

import jax
import jax.numpy as jnp
from jax import lax
from jax.experimental import pallas as pl
from jax.experimental.pallas import tpu as pltpu

F32 = jnp.float32
BF16 = jnp.bfloat16

RMS_EPS = 1e-6
GLA_GATE_TAU = 16.0
LANES = 128
MXU_DIM = 256
VMEM_LIMIT_BYTES = 52 * 1024 * 1024


def _split_bf16(x):
    hi = x.astype(BF16)
    lo = (x - hi.astype(F32)).astype(BF16)
    return hi, lo


def _dot(a, b):
    return jnp.dot(a, b, preferred_element_type=F32)


def _dot_nt(a, b):
    return lax.dot_general(a, b, (((1,), (1,)), ((), ())), preferred_element_type=F32)


def _dot_tn(a, b):
    return lax.dot_general(a, b, (((0,), (0,)), ((), ())), preferred_element_type=F32)


def _softplus(z):
    return jnp.maximum(z, 0.0) + jnp.log(1.0 + jnp.exp(-jnp.abs(z)))


def _sigmoid(z):
    return 1.0 / (1.0 + jnp.exp(-z))


def _proj_kernel(x_ref, g1_ref, w_ref, walr_ref, qg_ref, kg_ref, gsum_ref, qks_ref, bm_ref,
                 p_ref, k_ref, v_ref, alr_ref, xn_scr, *, d_head, sb_scale):
    j = pl.program_id(1)
    d_model = x_ref.shape[1]
    gw = gsum_ref.shape[0]

    @pl.when(j == 0)
    def _():
        x = x_ref[...]
        y = x * lax.rsqrt(jnp.mean(x * x, axis=-1, keepdims=True) + RMS_EPS) * g1_ref[...]
        xn = y.astype(BF16)
        xn_scr[...] = xn
        alr_ref[...] = _dot(xn, walr_ref[...])

    acc = _dot(xn_scr[...], w_ref[...])

    def head_norm(a, g):
        outs = []
        for c0 in range(0, d_model, gw):
            ac = a[:, c0:c0 + gw]
            hi, lo = _split_bf16(ac * ac)
            ss = _dot(hi, gsum_ref[...]) + _dot(lo, gsum_ref[...])
            outs.append(ac * lax.rsqrt(ss * (1.0 / d_head) + RMS_EPS))
        return jnp.concatenate(outs, axis=1) * g

    @pl.when(j == 0)
    def _():
        p_ref[...] = (head_norm(acc, qg_ref[...]) * sb_scale).astype(BF16)

    @pl.when(j == 1)
    def _():
        kf = head_norm(acc, kg_ref[...])
        k_ref[...] = kf
        p_ref[...] = kf.astype(BF16)

    @pl.when(j == 2)
    def _():
        v_ref[...] = acc
        p_ref[...] = acc.astype(BF16)

    @pl.when(j == 3)
    def _():
        p_ref[...] = (acc * qks_ref[...]).astype(BF16)

    @pl.when(j == 4)
    def _():
        p_ref[...] = acc.astype(BF16)

    @pl.when(j == 5)
    def _():
        p_ref[...] = (acc * _sigmoid(acc)).astype(BF16)

    @pl.when(j == 6)
    def _():
        p_ref[...] = _sigmoid(acc + bm_ref[0:1, :]).astype(BF16)

    @pl.when(j == 7)
    def _():
        p_ref[...] = _sigmoid(acc + bm_ref[1:2, :]).astype(BF16)


def _proj(x, g1, w, walr, qg, kg, gsum, qks, bm, *, tm, d_head):
    n, d = x.shape
    nblk = w.shape[1] // d
    const = lambda i, j: (0, 0)
    return pl.pallas_call(
        functools.partial(_proj_kernel, d_head=d_head, sb_scale=d_head ** -0.5),
        grid=(n // tm, nblk),
        in_specs=[
            pl.BlockSpec((tm, d), lambda i, j: (i, 0)),
            pl.BlockSpec((1, d), const),
            pl.BlockSpec((d, d), lambda i, j: (0, j)),
            pl.BlockSpec((d, LANES), const),
            pl.BlockSpec((1, d), const),
            pl.BlockSpec((1, d), const),
            pl.BlockSpec(gsum.shape, const),
            pl.BlockSpec((1, d), const),
            pl.BlockSpec((2, d), const),
        ],
        out_specs=[
            pl.BlockSpec((tm, d), lambda i, j: (i, j)),
            pl.BlockSpec((tm, d), lambda i, j: (i, 0)),
            pl.BlockSpec((tm, d), lambda i, j: (i, 0)),
            pl.BlockSpec((tm, LANES), lambda i, j: (i, 0)),
        ],
        out_shape=[
            jax.ShapeDtypeStruct((n, nblk * d), BF16),
            jax.ShapeDtypeStruct((n, d), F32),
            jax.ShapeDtypeStruct((n, d), F32),
            jax.ShapeDtypeStruct((n, LANES), F32),
        ],
        scratch_shapes=[pltpu.VMEM((tm, d), BF16)],
        compiler_params=pltpu.CompilerParams(
            dimension_semantics=("arbitrary", "arbitrary"), vmem_limit_bytes=VMEM_LIMIT_BYTES),
        name="proj",
    )(x, g1, w, walr, qg, kg, gsum, qks, bm)


def _sb_prompt_kernel(bias_ref, q_ref, k_ref, v_ref, tri_ref, o_ref, acc_scr, c_scr, *, tq, d_head):
    pair = pl.program_id(1)
    i = pl.program_id(2)
    lane = lax.broadcasted_iota(jnp.int32, (tq, 2 * d_head), 1)
    row = lax.broadcasted_iota(jnp.int32, (tq, tq), 0)
    col = lax.broadcasted_iota(jnp.int32, (tq, tq), 1)
    below_diag = col < row
    q = q_ref[...]
    qh = [jnp.where(lane < d_head, q, jnp.zeros_like(q)), jnp.where(lane >= d_head, q, jnp.zeros_like(q))]
    bias = [bias_ref[2 * pair], bias_ref[2 * pair + 1]]

    acc_scr[...] = jnp.zeros_like(acc_scr)
    c_scr[...] = jnp.zeros_like(c_scr)

    def step(j0, masked):
        kb = k_ref[pl.ds(j0, tq), :]
        vb = v_ref[pl.ds(j0, tq), :]
        for h in range(2):
            z = _dot_nt(qh[h], kb) + bias[h]
            sp = _softplus(z)
            ls = -sp
            if masked:
                ls = jnp.where(below_diag, ls, 0.0)
            r = _dot(ls.astype(BF16), tri_ref[...])
            c = c_scr[h]
            w = jnp.exp(z - sp + r + c)
            if masked:
                w = jnp.where(below_diag, w, 0.0)
            acc_scr[h] += _dot(w.astype(BF16), vb)
            c_scr[h] = c + jnp.sum(ls, axis=1, keepdims=True)

    step(pl.multiple_of(i * tq, tq), True)

    def body(n, carry):
        step(pl.multiple_of((i - 1 - n) * tq, tq), False)
        return carry

    lax.fori_loop(0, i, body, 0)
    o_ref[...] = jnp.where(lane < d_head, acc_scr[0], acc_scr[1]).astype(o_ref.dtype)


def _sb_prompt(p, bias, tri, *, batch, seq, n_heads, d_head, tq):
    n = batch * seq
    d = n_heads * d_head
    pw = 2 * d_head
    npair = n_heads // 2
    nq = seq // tq
    return pl.pallas_call(
        functools.partial(_sb_prompt_kernel, tq=tq, d_head=d_head),
        grid_spec=pltpu.PrefetchScalarGridSpec(
            num_scalar_prefetch=1,
            grid=(batch, npair, nq),
            in_specs=[
                pl.BlockSpec((tq, pw), lambda b, h, i, s: (b * nq + i, h)),
                pl.BlockSpec((seq, pw), lambda b, h, i, s: (b, npair + h)),
                pl.BlockSpec((seq, pw), lambda b, h, i, s: (b, 2 * npair + h)),
                pl.BlockSpec((tq, tq), lambda b, h, i, s: (0, 0)),
            ],
            out_specs=pl.BlockSpec((tq, pw), lambda b, h, i, s: (b * nq + i, h)),
            scratch_shapes=[pltpu.VMEM((2, tq, pw), F32), pltpu.VMEM((2, tq, 1), F32)],
        ),
        out_shape=jax.ShapeDtypeStruct((n, d), BF16),
        compiler_params=pltpu.CompilerParams(
            dimension_semantics=("arbitrary", "arbitrary", "arbitrary"), vmem_limit_bytes=VMEM_LIMIT_BYTES),
        name="sb_prompt",
    )(bias, p, p, p, tri)


def _sb_paged_kernel(pt_ref, q_ref, kn_ref, vn_ref, kc_ref, vc_ref, bc_ref, tri_ref, o_ref,
                     acc_scr, c_scr, qm_scr, *, n_heads, d_head, tq):
    p = pl.program_id(1)
    nl = n_heads * tq
    d = n_heads * d_head
    page = tri_ref.shape[0]
    row_head = lax.broadcasted_iota(jnp.int32, (nl, d), 0) // tq
    col_head = lax.broadcasted_iota(jnp.int32, (nl, d), 1) // d_head
    same_head = row_head == col_head

    def weights(z, mask):
        sp = _softplus(z)
        ls = -sp
        if mask is not None:
            ls = jnp.where(mask, ls, 0.0)
        r = _dot(ls.astype(BF16), tri_ref[...])
        w = jnp.exp(z - sp + r + c_scr[...])
        if mask is not None:
            w = jnp.where(mask, w, 0.0)
        c_scr[...] += jnp.sum(ls, axis=1, keepdims=True)
        return w.astype(BF16)

    @pl.when(p == 0)
    def _():
        q = q_ref[...]
        qm = jnp.where(same_head, jnp.concatenate([q] * n_heads, axis=0), 0.0).astype(BF16)
        qm_scr[...] = qm
        c_scr[...] = jnp.zeros_like(c_scr)
        pad = jnp.zeros((page - tq, d), F32)
        kn = jnp.concatenate([kn_ref[...], pad], axis=0).astype(BF16)
        vn = jnp.concatenate([vn_ref[...], pad], axis=0).astype(BF16)
        s_idx = lax.broadcasted_iota(jnp.int32, (nl, page), 1)
        t_idx = lax.broadcasted_iota(jnp.int32, (nl, page), 0) % tq
        w = weights(_dot_nt(qm, kn) + bc_ref[...], s_idx < t_idx)
        acc_scr[...] = _dot(w, vn)

    z = _dot(qm_scr[...], kc_ref[...].astype(BF16)) + bc_ref[...]
    w = weights(z, None)
    acc_scr[...] += _dot_nt(w, vc_ref[...].astype(BF16))

    @pl.when(p == pl.num_programs(1) - 1)
    def _():
        a = jnp.where(same_head, acc_scr[...], 0.0)
        out = a[0:tq, :]
        for h in range(1, n_heads):
            out = out + a[h * tq:(h + 1) * tq, :]
        o_ref[...] = out.astype(o_ref.dtype)


def _sb_paged(q, k_new, v_new, cache_kt, cache_vt, page_table, bias_col, tri, *, layer, n_heads, d_head, tq):
    n, d = q.shape
    bs, n_pages = page_table.shape
    page = cache_kt.shape[3]
    nl = n_heads * tq
    tok = lambda b, p, pt: (b, 0)
    cache = lambda b, p, pt: (layer, pt[b, n_pages - 1 - p], 0, 0)
    const = lambda b, p, pt: (0, 0)
    return pl.pallas_call(
        functools.partial(_sb_paged_kernel, n_heads=n_heads, d_head=d_head, tq=tq),
        grid_spec=pltpu.PrefetchScalarGridSpec(
            num_scalar_prefetch=1,
            grid=(bs, n_pages),
            in_specs=[
                pl.BlockSpec((tq, d), tok),
                pl.BlockSpec((tq, d), tok),
                pl.BlockSpec((tq, d), tok),
                pl.BlockSpec((None, None, d, page), cache),
                pl.BlockSpec((None, None, d, page), cache),
                pl.BlockSpec((nl, 1), const),
                pl.BlockSpec((page, page), const),
            ],
            out_specs=pl.BlockSpec((tq, d), tok),
            scratch_shapes=[pltpu.VMEM((nl, d), F32), pltpu.VMEM((nl, 1), F32), pltpu.VMEM((nl, d), BF16)],
        ),
        out_shape=jax.ShapeDtypeStruct((n, d), F32),
        compiler_params=pltpu.CompilerParams(
            dimension_semantics=("arbitrary", "arbitrary"), vmem_limit_bytes=VMEM_LIMIT_BYTES),
        name="sb_paged",
    )(page_table, q, k_new, v_new, cache_kt, cache_vt, bias_col, tri)


def _gla_kernel(qk_ref, v_ref, alr_ref, wa_ref, ba_ref, s0_ref, g_ref, cm_ref, o_ref, sfin_ref, s_scr,
                *, nseq, tb, chunk, sub, n_heads, dk, dv):
    t_blk = pl.program_id(1)
    kw = n_heads * dk
    n_sub = chunk // sub

    @pl.when(t_blk == 0)
    def _():
        s_scr[...] = s0_ref[...]

    ti = lax.broadcasted_iota(jnp.int32, (chunk, chunk), 0)
    si = lax.broadcasted_iota(jnp.int32, (chunk, chunk), 1)
    lag = ti // sub - si // sub
    diag_mask = (lag == 0) & (si <= ti)
    eye = (lax.broadcasted_iota(jnp.int32, (dk, dk), 0) == lax.broadcasted_iota(jnp.int32, (dk, dk), 1))

    def one_chunk(seq, r0):
        rows = pl.ds(r0, chunk)
        a_hi, a_lo = _split_bf16(alr_ref[rows, :])
        w_hi, w_lo = _split_bf16(wa_ref[...])
        pre = _dot(a_hi, w_hi) + _dot(a_lo, w_hi) + _dot(a_hi, w_lo) + ba_ref[...]
        log_a = (jnp.minimum(pre, 0.0) - jnp.log(1.0 + jnp.exp(-jnp.abs(pre)))) * (1.0 / GLA_GATE_TAU)
        l_hi, l_lo = _split_bf16(log_a)
        stats = _dot(cm_ref[...], l_hi) + _dot(cm_ref[...], l_lo)
        b_all = stats[0 * chunk:1 * chunk]
        rs_all = stats[1 * chunk:2 * chunk]
        re_all = stats[2 * chunk:3 * chunk]
        mid_all = stats[3 * chunk:4 * chunk]
        last_all = stats[4 * chunk:5 * chunk]
        qk = qk_ref[rows, :].astype(F32)
        vv = v_ref[rows, :]
        for h in range(n_heads):
            ks = slice(h * dk, (h + 1) * dk)
            q = qk[:, h * dk:(h + 1) * dk]
            k = qk[:, kw + h * dk:kw + (h + 1) * dk]
            v = vv[:, h * dv:(h + 1) * dv]
            b, rs, re, mid, last = b_all[:, ks], rs_all[:, ks], re_all[:, ks], mid_all[:, ks], last_all[:, ks]
            q_dec = q * jnp.exp(b - rs)
            k_dec = (k * jnp.exp(re - b)).astype(BF16)
            sc = jnp.where(diag_mask,
                           _dot_nt((q * jnp.exp(b - mid)).astype(BF16), (k * jnp.exp(mid - b)).astype(BF16)), 0.0)
            for d_lag in range(1, n_sub):
                ql = q_dec
                if d_lag > 1:
                    shift = sub * (d_lag - 1)
                    rs_prev = jnp.concatenate([jnp.zeros((shift, dk), F32), rs[:chunk - shift]], axis=0)
                    ql = q_dec * jnp.exp(rs - rs_prev)
                sc = sc + jnp.where(lag == d_lag, _dot_nt(ql.astype(BF16), k_dec), 0.0)
            s_old = s_scr[seq, h]
            o = _dot((q_dec * jnp.exp(rs)).astype(BF16), s_old.astype(BF16)) + _dot(sc.astype(BF16), v)
            k_st = (k * jnp.exp(last - b)).astype(BF16)
            decay_col = jnp.sum(jnp.where(eye, jnp.exp(last[0:1, :]), 0.0), axis=1, keepdims=True)
            s_scr[seq, h] = s_old * decay_col + _dot_tn(k_st, v)
            on = o * lax.rsqrt(jnp.mean(o * o, axis=-1, keepdims=True) + RMS_EPS) * g_ref[...]
            o_ref[rows, h * dv:(h + 1) * dv] = on.astype(o_ref.dtype)

    n_chunks = tb // chunk
    for seq in range(nseq):
        if n_chunks == 1:
            one_chunk(seq, seq * tb)
        else:
            def body(n, carry, seq=seq):
                one_chunk(seq, pl.multiple_of(seq * tb + n * chunk, chunk))
                return carry
            lax.fori_loop(0, n_chunks, body, 0)

    @pl.when(t_blk == pl.num_programs(1) - 1)
    def _():
        sfin_ref[...] = s_scr[...]


def _gla(p, alr, wa, ba, s0, g, cm, *, batch, seq, nseq, tb, chunk, sub, d_model):
    n = batch * seq
    _, n_heads, dk, dv = s0.shape
    kw = n_heads * dk
    vw = n_heads * dv
    nt = seq // tb
    rows = nseq * tb
    tok = lambda b, t: (b * nt + t, 0)
    const = lambda b, t: (0, 0)
    return pl.pallas_call(
        functools.partial(_gla_kernel, nseq=nseq, tb=tb, chunk=chunk, sub=sub, n_heads=n_heads, dk=dk, dv=dv),
        grid=(batch // nseq, nt),
        in_specs=[
            pl.BlockSpec((rows, 2 * kw), lambda b, t: (b * nt + t, 3 * d_model // (2 * kw))),
            pl.BlockSpec((rows, vw), lambda b, t: (b * nt + t, 4 * d_model // vw)),
            pl.BlockSpec((rows, LANES), tok),
            pl.BlockSpec((LANES, kw), const),
            pl.BlockSpec((1, kw), const),
            pl.BlockSpec((nseq, n_heads, dk, dv), lambda b, t: (b, 0, 0, 0)),
            pl.BlockSpec((1, dv), const),
            pl.BlockSpec(cm.shape, const),
        ],
        out_specs=[
            pl.BlockSpec((rows, vw), tok),
            pl.BlockSpec((nseq, n_heads, dk, dv), lambda b, t: (b, 0, 0, 0)),
        ],
        out_shape=[
            jax.ShapeDtypeStruct((n, vw), BF16),
            jax.ShapeDtypeStruct(s0.shape, F32),
        ],
        scratch_shapes=[pltpu.VMEM((nseq, n_heads, dk, dv), F32)],
        compiler_params=pltpu.CompilerParams(
            dimension_semantics=("arbitrary", "arbitrary"), vmem_limit_bytes=VMEM_LIMIT_BYTES),
        name="gla",
    )(p, p, alr, wa, ba, s0, g, cm)


def _chunk_matrices(chunk, sub):
    t = jnp.arange(chunk)[:, None]
    j = jnp.arange(chunk)[None, :]
    s0 = (t // sub) * sub
    mats = [j <= t, j < s0, j < s0 + sub, j < s0 + sub // 2, j < chunk + 0 * t]
    return jnp.concatenate(mats, axis=0).astype(BF16)


def _mlp_kernel(x_ref, oa_ref, ob_ref, rb_ref, ga_ref, gb_ref, wo_ref, g2_ref, wu_ref, wd_ref, y_ref,
                hn_scr, acc_scr):
    f = pl.program_id(1)

    @pl.when(f == 0)
    def _():
        mixed = (ga_ref[...].astype(F32) * oa_ref[...].astype(F32)
                 + gb_ref[...].astype(F32) * (ob_ref[...].astype(F32) * rb_ref[...].astype(F32)))
        h = x_ref[...] + _dot(mixed.astype(BF16), wo_ref[...])
        acc_scr[...] = h
        hn = h * lax.rsqrt(jnp.mean(h * h, axis=-1, keepdims=True) + RMS_EPS) * g2_ref[...]
        hn_scr[...] = hn.astype(BF16)

    u = jnp.maximum(_dot(hn_scr[...], wu_ref[...]), 0.0)
    acc_scr[...] += _dot((u * u).astype(BF16), wd_ref[...])

    @pl.when(f == pl.num_programs(1) - 1)
    def _():
        y_ref[...] = acc_scr[...]


def _mlp(x, oa, ob, p, wo, g2, wu, wd, *, tm, tf):
    n, d = x.shape
    dff = wu.shape[1]
    row = lambda i, f: (i, 0)
    const = lambda i, f: (0, 0)
    return pl.pallas_call(
        _mlp_kernel,
        grid=(n // tm, dff // tf),
        in_specs=[
            pl.BlockSpec((tm, d), row),
            pl.BlockSpec((tm, d), row),
            pl.BlockSpec((tm, d), row),
            pl.BlockSpec((tm, d), lambda i, f: (i, 5)),
            pl.BlockSpec((tm, d), lambda i, f: (i, 6)),
            pl.BlockSpec((tm, d), lambda i, f: (i, 7)),
            pl.BlockSpec((d, d), const),
            pl.BlockSpec((1, d), const),
            pl.BlockSpec((d, tf), lambda i, f: (0, f)),
            pl.BlockSpec((tf, d), lambda i, f: (f, 0)),
        ],
        out_specs=pl.BlockSpec((tm, d), row),
        out_shape=jax.ShapeDtypeStruct((n, d), F32),
        scratch_shapes=[pltpu.VMEM((tm, d), BF16), pltpu.VMEM((tm, d), F32)],
        compiler_params=pltpu.CompilerParams(
            dimension_semantics=("arbitrary", "arbitrary"), vmem_limit_bytes=VMEM_LIMIT_BYTES),
        name="mlp",
    )(x, oa, ob, p, p, p, wo, g2, wu, wd)


def _pick(n, target):
    t = min(n, target)
    while n % t:
        t //= 2
    return t


def kernel(x_prompt, x_sample, cache_k, cache_v, state_gla, page_table, norm1_g, w_in, q_norm_g, k_norm_g,
           sb_bias, w_alpha2, b_alpha, gla_norm_g, b_merge, w_out, norm2_g, w_up, w_down):
    bp, seq, d = x_prompt.shape
    bs, tq, _ = x_sample.shape
    depth, n_pool, page, n_heads, d_head = cache_k.shape
    _, _, g_heads, dk, dv = state_gla.shape
    rank = w_alpha2.shape[1]
    kw = g_heads * dk
    assert n_heads * d_head == d and g_heads * dv == d and 2 * kw == d and n_heads % 2 == 0
    assert 2 * d_head == LANES and rank <= LANES

    ck = jnp.transpose(cache_k, (0, 1, 3, 4, 2)).reshape(depth, n_pool, d, page)
    cv = jnp.transpose(cache_v, (0, 1, 3, 4, 2)).reshape(depth, n_pool, d, page)
    xp = x_prompt.reshape(bp * seq, d)
    xs = x_sample.reshape(bs * tq, d)

    gw = min(d, MXU_DIM)
    gi = jnp.arange(gw) // d_head
    gsum = (gi[:, None] == gi[None, :]).astype(BF16)
    tq_blk = _pick(seq, MXU_DIM)
    ki = jnp.arange(tq_blk)
    tri = (ki[:, None] > ki[None, :]).astype(BF16)
    pi = jnp.arange(page)
    tri_page = (pi[:, None] > pi[None, :]).astype(BF16)
    qks = jnp.concatenate([jnp.full((1, kw), dk ** -0.5, F32), jnp.ones((1, kw), F32)], axis=1)
    chunk_p, sub_p = 64, 16
    cm_p = _chunk_matrices(chunk_p, sub_p)
    cm_s = _chunk_matrices(tq, tq)
    s0_p = jnp.zeros((bp, g_heads, dk, dv), F32)

    o_qa, o_ka, o_va = 0, d, 2 * d
    o_qb, o_kb, o_vb, o_rb = 3 * d, 3 * d + kw, 3 * d + 2 * kw, 4 * d + 2 * kw
    o_alr = o_rb + d
    o_ga = o_alr + rank
    o_gb = o_ga + d

    yp, ys = xp, xs
    outs = [[] for _ in range(6)]
    for l in range(depth):
        wl = w_in[l]
        w_main = jnp.concatenate([wl[:, :o_alr], wl[:, o_ga:]], axis=1).astype(BF16)
        w_alr = jnp.pad(wl[:, o_alr:o_ga], ((0, 0), (0, LANES - rank))).astype(BF16)
        g1 = norm1_g[l].reshape(1, d)
        qg = jnp.tile(q_norm_g[l], n_heads).reshape(1, d)
        kg = jnp.tile(k_norm_g[l], n_heads).reshape(1, d)
        wa = jnp.pad(w_alpha2[l], ((0, LANES - rank), (0, 0)))
        ba = b_alpha[l].reshape(1, kw)
        gg = gla_norm_g[l].reshape(1, dv)
        bm = b_merge[l]
        wo = w_out[l].astype(BF16)
        g2 = norm2_g[l].reshape(1, d)
        wu = w_up[l].astype(BF16)
        wd = w_down[l].astype(BF16)
        bias_col = jnp.repeat(sb_bias[l], tq).reshape(n_heads * tq, 1)

        pp, kp, vp, alr_p = _proj(yp, g1, w_main, w_alr, qg, kg, gsum, qks, bm, tm=_pick(bp * seq, 512),
                                  d_head=d_head)
        oa_p = _sb_prompt(pp, sb_bias[l], tri, batch=bp, seq=seq, n_heads=n_heads, d_head=d_head, tq=tq_blk)
        ob_p, sp = _gla(pp, alr_p, wa, ba, s0_p, gg, cm_p, batch=bp, seq=seq, nseq=1, tb=_pick(seq, 512),
                        chunk=chunk_p, sub=sub_p, d_model=d)
        yp = _mlp(yp, oa_p, ob_p, pp, wo, g2, wu, wd, tm=_pick(bp * seq, 512), tf=_pick(w_up.shape[2], 1024))

        ps, ksm, vsm, alr_s = _proj(ys, g1, w_main, w_alr, qg, kg, gsum, qks, bm, tm=_pick(bs * tq, 256),
                                    d_head=d_head)
        q_s = ps[:, :d].astype(F32)
        oa_s = _sb_paged(q_s, ksm, vsm, ck, cv, page_table, bias_col, tri_page, layer=l, n_heads=n_heads,
                         d_head=d_head, tq=tq)
        ob_s, ss = _gla(ps, alr_s, wa, ba, state_gla[l], gg, cm_s, batch=bs, seq=tq, nseq=2, tb=tq,
                        chunk=tq, sub=tq, d_model=d)
        ys = _mlp(ys, oa_s, ob_s, ps, wo, g2, wu, wd, tm=_pick(bs * tq, 256), tf=_pick(w_up.shape[2], 1024))

        for lst, val in zip(outs, (kp, vp, sp, ksm, vsm, ss)):
            lst.append(val)

    k_prompt = jnp.stack(outs[0]).reshape(depth, bp, seq, n_heads, d_head)
    v_prompt = jnp.stack(outs[1]).reshape(depth, bp, seq, n_heads, d_head)
    gla_prompt = jnp.stack(outs[2])
    k_sample = jnp.stack(outs[3]).reshape(depth, bs, tq, n_heads, d_head)
    v_sample = jnp.stack(outs[4]).reshape(depth, bs, tq, n_heads, d_head)
    gla_sample = jnp.stack(outs[5])
    return (yp.reshape(bp, seq, d), ys.reshape(bs, tq, d), k_prompt, v_prompt, gla_prompt,
            k_sample, v_sample, gla_sample)
```

```python
import functools

import jax
import jax.numpy as jnp
from jax import lax
from jax.experimental import pallas as pl
from jax.experimental.pallas import tpu as pltpu

F32 = jnp.float32
BF16 = jnp.bfloat16

RMS_EPS = 1e-6
GLA_GATE_TAU = 16.0
LANES = 128
MXU_DIM = 256
VMEM_LIMIT_BYTES = 52 * 1024 * 1024
SB_QUERY_BLOCK = 1024
SB_PAGES_PER_STEP = 8


def _split_bf16(x):
    hi = x.astype(BF16)
    lo = (x - hi.astype(F32)).astype(BF16)
    return hi, lo


def _dot(a, b):
    return jnp.dot(a, b, preferred_element_type=F32)


def _dot_nt(a, b):
    return lax.dot_general(a, b, (((1,), (1,)), ((), ())), preferred_element_type=F32)


def _dot_tn(a, b):
    return lax.dot_general(a, b, (((0,), (0,)), ((), ())), preferred_element_type=F32)


def _neg_abs(z):
    bits = lax.bitcast_convert_type(z, jnp.uint32) | jnp.uint32(0x80000000)
    return lax.bitcast_convert_type(bits, F32)


def _softplus(z):
    return jnp.maximum(z, 0.0) + jnp.log(1.0 + jnp.exp(_neg_abs(z)))


def _sigmoid(z):
    return 1.0 / (1.0 + jnp.exp(-z))


def _proj_kernel(x_ref, g1_ref, w_ref, walr_ref, qg_ref, kg_ref, gsum_ref, qks_ref, bm_ref,
                 p_ref, k_ref, v_ref, alr_ref, xn_scr, *, d_head, sb_scale):
    j = pl.program_id(1)
    d_model = x_ref.shape[1]
    gw = gsum_ref.shape[0]

    @pl.when(j == 0)
    def _():
        x = x_ref[...]
        y = x * lax.rsqrt(jnp.mean(x * x, axis=-1, keepdims=True) + RMS_EPS) * g1_ref[...]
        xn = y.astype(BF16)
        xn_scr[...] = xn
        alr_ref[...] = _dot(xn, walr_ref[...])

    def proj():
        return _dot(xn_scr[...], w_ref[...])

    def head_norm(a, g):
        outs = []
        for c0 in range(0, d_model, gw):
            ac = a[:, c0:c0 + gw]
            hi, lo = _split_bf16(ac * ac)
            ss = _dot(hi, gsum_ref[...]) + _dot(lo, gsum_ref[...])
            outs.append(ac * lax.rsqrt(ss * (1.0 / d_head) + RMS_EPS))
        return jnp.concatenate(outs, axis=1) * g

    @pl.when(j == 0)
    def _():
        p_ref[...] = (head_norm(proj(), qg_ref[...]) * sb_scale).astype(BF16)

    @pl.when(j == 1)
    def _():
        kf = head_norm(proj(), kg_ref[...])
        k_ref[...] = kf
        p_ref[...] = kf.astype(BF16)

    @pl.when(j == 2)
    def _():
        acc = proj()
        v_ref[...] = acc
        p_ref[...] = acc.astype(BF16)

    @pl.when(j == 3)
    def _():
        p_ref[...] = (proj() * qks_ref[...]).astype(BF16)

    @pl.when(j == 4)
    def _():
        p_ref[...] = proj().astype(BF16)

    @pl.when(j == 5)
    def _():
        acc = proj()
        p_ref[...] = (acc * _sigmoid(acc)).astype(BF16)

    @pl.when(j == 6)
    def _():
        p_ref[...] = _sigmoid(proj() + bm_ref[0:1, :]).astype(BF16)

    @pl.when(j == 7)
    def _():
        p_ref[...] = _sigmoid(proj() + bm_ref[1:2, :]).astype(BF16)


def _proj(x, g1, w, walr, qg, kg, gsum, qks, bm, *, tm, d_head):
    n, d = x.shape
    nblk = w.shape[1] // d
    const = lambda i, j: (0, 0)
    return pl.pallas_call(
        functools.partial(_proj_kernel, d_head=d_head, sb_scale=d_head ** -0.5),
        grid=(n // tm, nblk),
        in_specs=[
            pl.BlockSpec((tm, d), lambda i, j: (i, 0)),
            pl.BlockSpec((1, d), const),
            pl.BlockSpec((d, d), lambda i, j: (0, j)),
            pl.BlockSpec((d, LANES), const),
            pl.BlockSpec((1, d), const),
            pl.BlockSpec((1, d), const),
            pl.BlockSpec(gsum.shape, const),
            pl.BlockSpec((1, d), const),
            pl.BlockSpec((2, d), const),
        ],
        out_specs=[
            pl.BlockSpec((tm, d), lambda i, j: (i, j)),
            pl.BlockSpec((tm, d), lambda i, j: (i, 0)),
            pl.BlockSpec((tm, d), lambda i, j: (i, 0)),
            pl.BlockSpec((tm, LANES), lambda i, j: (i, 0)),
        ],
        out_shape=[
            jax.ShapeDtypeStruct((n, nblk * d), BF16),
            jax.ShapeDtypeStruct((n, d), F32),
            jax.ShapeDtypeStruct((n, d), F32),
            jax.ShapeDtypeStruct((n, LANES), F32),
        ],
        scratch_shapes=[pltpu.VMEM((tm, d), BF16)],
        compiler_params=pltpu.CompilerParams(
            dimension_semantics=("arbitrary", "arbitrary"), vmem_limit_bytes=VMEM_LIMIT_BYTES),
        name="proj",
    )(x, g1, w, walr, qg, kg, gsum, qks, bm)


def _sb_prompt_kernel(bias_ref, q_ref, k_ref, v_ref, tri_ref, o_ref, acc_scr, c_scr, *, tq, tk, d_head):
    pair = pl.program_id(1)
    i = pl.program_id(2)
    n_sub = tq // tk
    lane = lax.broadcasted_iota(jnp.int32, (tq, 2 * d_head), 1)
    q = q_ref[...]
    qh = [jnp.where(lane < d_head, q, jnp.zeros_like(q)), jnp.where(lane >= d_head, q, jnp.zeros_like(q))]
    bias = [bias_ref[2 * pair], bias_ref[2 * pair + 1]]

    acc_scr[...] = jnp.zeros_like(acc_scr)
    c_scr[...] = jnp.zeros_like(c_scr)

    def step(j0, r0, masked):
        kb = k_ref[pl.ds(j0, tk), :]
        vb = v_ref[pl.ds(j0, tk), :]
        if masked:
            visible = (lax.broadcasted_iota(jnp.int32, (tq - r0, tk), 1)
                       < lax.broadcasted_iota(jnp.int32, (tq - r0, tk), 0))
        for h in range(2):
            z = _dot_nt(qh[h][r0:], kb) + bias[h]
            t = _softplus(z)
            if masked:
                t = jnp.where(visible, t, 0.0)
            r = _dot(t.astype(BF16), tri_ref[...])
            c = c_scr[h, r0:, :]
            w = jnp.exp(z - t - r - c)
            if masked:
                w = jnp.where(visible, w, 0.0)
            acc_scr[h, r0:, :] += _dot(w.astype(BF16), vb)
            c_scr[h, r0:, :] = c + (r[:, 0:1] + t[:, 0:1])

    for dj in reversed(range(n_sub)):
        step(pl.multiple_of(i * tq + dj * tk, tk), dj * tk, True)

    def body(n, carry):
        step(pl.multiple_of((i * n_sub - 1 - n) * tk, tk), 0, False)
        return carry

    lax.fori_loop(0, i * n_sub, body, 0)
    o_ref[...] = jnp.where(lane < d_head, acc_scr[0], acc_scr[1]).astype(o_ref.dtype)


def _sb_prompt(p, bias, tri, *, batch, seq, n_heads, d_head, tq):
    n = batch * seq
    d = n_heads * d_head
    pw = 2 * d_head
    npair = n_heads // 2
    nq = seq // tq
    tk = tri.shape[0]
    return pl.pallas_call(
        functools.partial(_sb_prompt_kernel, tq=tq, tk=tk, d_head=d_head),
        grid_spec=pltpu.PrefetchScalarGridSpec(
            num_scalar_prefetch=1,
            grid=(batch, npair, nq),
            in_specs=[
                pl.BlockSpec((tq, pw), lambda b, h, i, s: (b * nq + i, h)),
                pl.BlockSpec((seq, pw), lambda b, h, i, s: (b, npair + h)),
                pl.BlockSpec((seq, pw), lambda b, h, i, s: (b, 2 * npair + h)),
                pl.BlockSpec((tk, tk), lambda b, h, i, s: (0, 0)),
            ],
            out_specs=pl.BlockSpec((tq, pw), lambda b, h, i, s: (b * nq + i, h)),
            scratch_shapes=[pltpu.VMEM((2, tq, pw), F32), pltpu.VMEM((2, tq, 1), F32)],
        ),
        out_shape=jax.ShapeDtypeStruct((n, d), BF16),
        compiler_params=pltpu.CompilerParams(
            dimension_semantics=("arbitrary", "arbitrary", "arbitrary"), vmem_limit_bytes=VMEM_LIMIT_BYTES),
        name="sb_prompt",
    )(bias, p, p, p, tri)


def _sb_paged_kernel(pt_ref, q_ref, kn_ref, vn_ref, *refs, n_heads, d_head, tq, n_grp):
    kc_refs = refs[:n_grp]
    vc_refs = refs[n_grp:2 * n_grp]
    bc_ref, tri_ref, o_ref, acc_scr, c_scr, qm_scr = refs[2 * n_grp:]
    g = pl.program_id(1)
    nl = n_heads * tq
    d = n_heads * d_head
    tw = tri_ref.shape[0]
    row_head = lax.broadcasted_iota(jnp.int32, (nl, d), 0) // tq
    col_head = lax.broadcasted_iota(jnp.int32, (nl, d), 1) // d_head
    same_head = row_head == col_head

    def weights(z, mask):
        t = _softplus(z)
        if mask is not None:
            t = jnp.where(mask, t, 0.0)
        c = c_scr[...]
        n_chunks = z.shape[1] // tw
        ws = [None] * n_chunks
        for ci in reversed(range(n_chunks)):
            cols = slice(ci * tw, (ci + 1) * tw)
            tc = t[:, cols]
            r = _dot(tc.astype(BF16), tri_ref[...])
            w = jnp.exp(z[:, cols] - tc - r - c)
            if mask is not None:
                w = jnp.where(mask[:, cols], w, 0.0)
            ws[ci] = w.astype(BF16)
            c = c + jnp.sum(tc, axis=1, keepdims=True)
        c_scr[...] = c
        return ws[0] if n_chunks == 1 else jnp.concatenate(ws, axis=1)

    @pl.when(g == 0)
    def _():
        q = q_ref[...]
        qm = jnp.where(same_head, jnp.concatenate([q] * n_heads, axis=0), 0.0).astype(BF16)
        qm_scr[...] = qm
        c_scr[...] = jnp.zeros_like(c_scr)
        pad = jnp.zeros((tw - tq, d), F32)
        kn = jnp.concatenate([kn_ref[...], pad], axis=0).astype(BF16)
        vn = jnp.concatenate([vn_ref[...], pad], axis=0).astype(BF16)
        s_idx = lax.broadcasted_iota(jnp.int32, (nl, tw), 1)
        t_idx = lax.broadcasted_iota(jnp.int32, (nl, tw), 0) % tq
        w = weights(_dot_nt(qm, kn) + bc_ref[...], s_idx < t_idx)
        acc_scr[...] = _dot(w, vn)

    kt = jnp.concatenate([r[...].astype(BF16) for r in kc_refs], axis=1)
    w = weights(_dot(qm_scr[...], kt) + bc_ref[...], None)
    vt = jnp.concatenate([r[...].astype(BF16) for r in vc_refs], axis=1)
    acc_scr[...] += _dot_nt(w, vt)

    @pl.when(g == pl.num_programs(1) - 1)
    def _():
        a = jnp.where(same_head, acc_scr[...], 0.0)
        out = a[0:tq, :]
        for h in range(1, n_heads):
            out = out + a[h * tq:(h + 1) * tq, :]
        o_ref[...] = out.astype(o_ref.dtype)


def _sb_paged(q, k_new, v_new, cache_kt, cache_vt, page_table, bias_col, tri, *, layer, n_heads, d_head, tq,
              n_grp):
    n, d = q.shape
    bs, n_pages = page_table.shape
    page = cache_kt.shape[3]
    nl = n_heads * tq
    assert (n_grp * page) % tri.shape[0] == 0 and n_pages % n_grp == 0
    tok = lambda b, g, pt: (b, 0)
    const = lambda b, g, pt: (0, 0)

    def cache(j):
        return pl.BlockSpec((None, None, d, page),
                            lambda b, g, pt: (layer, pt[b, n_pages - n_grp * (g + 1) + j], 0, 0))

    return pl.pallas_call(
        functools.partial(_sb_paged_kernel, n_heads=n_heads, d_head=d_head, tq=tq, n_grp=n_grp),
        grid_spec=pltpu.PrefetchScalarGridSpec(
            num_scalar_prefetch=1,
            grid=(bs, n_pages // n_grp),
            in_specs=([pl.BlockSpec((tq, d), tok)] * 3
                      + [cache(j) for j in range(n_grp)] * 2
                      + [pl.BlockSpec((nl, 1), const), pl.BlockSpec(tri.shape, const)]),
            out_specs=pl.BlockSpec((tq, d), tok),
            scratch_shapes=[pltpu.VMEM((nl, d), F32), pltpu.VMEM((nl, 1), F32), pltpu.VMEM((nl, d), BF16)],
        ),
        out_shape=jax.ShapeDtypeStruct((n, d), F32),
        compiler_params=pltpu.CompilerParams(
            dimension_semantics=("arbitrary", "arbitrary"), vmem_limit_bytes=VMEM_LIMIT_BYTES),
        name="sb_paged",
    )(page_table, q, k_new, v_new, *([cache_kt] * n_grp), *([cache_vt] * n_grp), bias_col, tri)


def _gla_kernel(qk_ref, v_ref, alr_ref, wa_ref, ba_ref, s0_ref, g_ref, cm_ref, o_ref, sfin_ref, s_scr,
                *, nseq, tb, chunk, sub, n_heads, dk, dv):
    t_blk = pl.program_id(1)
    kw = n_heads * dk
    n_sub = chunk // sub

    @pl.when(t_blk == 0)
    def _():
        s_scr[...] = s0_ref[...]

    ti = lax.broadcasted_iota(jnp.int32, (chunk, chunk), 0)
    si = lax.broadcasted_iota(jnp.int32, (chunk, chunk), 1)
    lag = ti // sub - si // sub
    diag_mask = (lag == 0) & (si <= ti)
    eye = (lax.broadcasted_iota(jnp.int32, (dk, dk), 0) == lax.broadcasted_iota(jnp.int32, (dk, dk), 1))

    def one_chunk(seq, r0):
        rows = pl.ds(r0, chunk)
        a_hi, a_lo = _split_bf16(alr_ref[rows, :])
        w_hi, w_lo = _split_bf16(wa_ref[...])
        pre = _dot(a_hi, w_hi) + _dot(a_lo, w_hi) + _dot(a_hi, w_lo) + ba_ref[...]
        log_a = (jnp.minimum(pre, 0.0) - jnp.log(1.0 + jnp.exp(-jnp.abs(pre)))) * (1.0 / GLA_GATE_TAU)
        l_hi, l_lo = _split_bf16(log_a)
        stats = _dot(cm_ref[...], l_hi) + _dot(cm_ref[...], l_lo)
        b_all = stats[0 * chunk:1 * chunk]
        rs_all = stats[1 * chunk:2 * chunk]
        re_all = stats[2 * chunk:3 * chunk]
        mid_all = stats[3 * chunk:4 * chunk]
        last_all = stats[4 * chunk:5 * chunk]
        qk = qk_ref[rows, :].astype(F32)
        vv = v_ref[rows, :]
        for h in range(n_heads):
            ks = slice(h * dk, (h + 1) * dk)
            q = qk[:, h * dk:(h + 1) * dk]
            k = qk[:, kw + h * dk:kw + (h + 1) * dk]
            v = vv[:, h * dv:(h + 1) * dv]
            b, rs, re, mid, last = b_all[:, ks], rs_all[:, ks], re_all[:, ks], mid_all[:, ks], last_all[:, ks]
            q_dec = q * jnp.exp(b - rs)
            k_dec = (k * jnp.exp(re - b)).astype(BF16)
            sc = jnp.where(diag_mask,
                           _dot_nt((q * jnp.exp(b - mid)).astype(BF16), (k * jnp.exp(mid - b)).astype(BF16)), 0.0)
            for d_lag in range(1, n_sub):
                ql = q_dec
                if d_lag > 1:
                    shift = sub * (d_lag - 1)
                    rs_prev = jnp.concatenate([jnp.zeros((shift, dk), F32), rs[:chunk - shift]], axis=0)
                    ql = q_dec * jnp.exp(rs - rs_prev)
                sc = sc + jnp.where(lag == d_lag, _dot_nt(ql.astype(BF16), k_dec), 0.0)
            s_old = s_scr[seq, h]
            o = _dot((q_dec * jnp.exp(rs)).astype(BF16), s_old.astype(BF16)) + _dot(sc.astype(BF16), v)
            k_st = (k * jnp.exp(last - b)).astype(BF16)
            decay_col = jnp.sum(jnp.where(eye, jnp.exp(last[0:1, :]), 0.0), axis=1, keepdims=True)
            s_scr[seq, h] = s_old * decay_col + _dot_tn(k_st, v)
            on = o * lax.rsqrt(jnp.mean(o * o, axis=-1, keepdims=True) + RMS_EPS) * g_ref[...]
            o_ref[rows, h * dv:(h + 1) * dv] = on.astype(o_ref.dtype)

    n_chunks = tb // chunk
    for seq in range(nseq):
        if n_chunks == 1:
            one_chunk(seq, seq * tb)
        else:
            unroll = 2 if n_chunks % 2 == 0 else 1

            def body(n, carry, seq=seq, unroll=unroll):
                for u in range(unroll):
                    one_chunk(seq, pl.multiple_of(seq * tb + (n * unroll + u) * chunk, chunk))
                return carry
            lax.fori_loop(0, n_chunks // unroll, body, 0)

    @pl.when(t_blk == pl.num_programs(1) - 1)
    def _():
        sfin_ref[...] = s_scr[...]


def _gla(p, alr, wa, ba, s0, g, cm, *, batch, seq, nseq, tb, chunk, sub, d_model):
    n = batch * seq
    _, n_heads, dk, dv = s0.shape
    kw = n_heads * dk
    vw = n_heads * dv
    nt = seq // tb
    rows = nseq * tb
    tok = lambda b, t: (b * nt + t, 0)
    const = lambda b, t: (0, 0)
    return pl.pallas_call(
        functools.partial(_gla_kernel, nseq=nseq, tb=tb, chunk=chunk, sub=sub, n_heads=n_heads, dk=dk, dv=dv),
        grid=(batch // nseq, nt),
        in_specs=[
            pl.BlockSpec((rows, 2 * kw), lambda b, t: (b * nt + t, 3 * d_model // (2 * kw))),
            pl.BlockSpec((rows, vw), lambda b, t: (b * nt + t, 4 * d_model // vw)),
            pl.BlockSpec((rows, LANES), tok),
            pl.BlockSpec((LANES, kw), const),
            pl.BlockSpec((1, kw), const),
            pl.BlockSpec((nseq, n_heads, dk, dv), lambda b, t: (b, 0, 0, 0)),
            pl.BlockSpec((1, dv), const),
            pl.BlockSpec(cm.shape, const),
        ],
        out_specs=[
            pl.BlockSpec((rows, vw), tok),
            pl.BlockSpec((nseq, n_heads, dk, dv), lambda b, t: (b, 0, 0, 0)),
        ],
        out_shape=[
            jax.ShapeDtypeStruct((n, vw), BF16),
            jax.ShapeDtypeStruct(s0.shape, F32),
        ],
        scratch_shapes=[pltpu.VMEM((nseq, n_heads, dk, dv), F32)],
        compiler_params=pltpu.CompilerParams(
            dimension_semantics=("arbitrary", "arbitrary"), vmem_limit_bytes=VMEM_LIMIT_BYTES),
        name="gla",
    )(p, p, alr, wa, ba, s0, g, cm)


def _chunk_matrices(chunk, sub):
    t = jnp.arange(chunk)[:, None]
    j = jnp.arange(chunk)[None, :]
    s0 = (t // sub) * sub
    mats = [j <= t, j < s0, j < s0 + sub, j < s0 + sub // 2, j < chunk + 0 * t]
    return jnp.concatenate(mats, axis=0).astype(BF16)


def _mlp_kernel(x_ref, oa_ref, ob_ref, rb_ref, ga_ref, gb_ref, wo_ref, g2_ref, wu_ref, wd_ref, y_ref,
                hn_scr, acc_scr):
    f = pl.program_id(1)

    @pl.when(f == 0)
    def _():
        mixed = (ga_ref[...].astype(F32) * oa_ref[...].astype(F32)
                 + gb_ref[...].astype(F32) * (ob_ref[...].astype(F32) * rb_ref[...].astype(F32)))
        h = x_ref[...] + _dot(mixed.astype(BF16), wo_ref[...])
        acc_scr[...] = h
        hn = h * lax.rsqrt(jnp.mean(h * h, axis=-1, keepdims=True) + RMS_EPS) * g2_ref[...]
        hn_scr[...] = hn.astype(BF16)

    u = jnp.maximum(_dot(hn_scr[...], wu_ref[...]), 0.0)
    acc_scr[...] += _dot((u * u).astype(BF16), wd_ref[...])

    @pl.when(f == pl.num_programs(1) - 1)
    def _():
        y_ref[...] = acc_scr[...]


def _mlp(x, oa, ob, p, wo, g2, wu, wd, *, tm, tf):
    n, d = x.shape
    dff = wu.shape[1]
    row = lambda i, f: (i, 0)
    const = lambda i, f: (0, 0)
    return pl.pallas_call(
        _mlp_kernel,
        grid=(n // tm, dff // tf),
        in_specs=[
            pl.BlockSpec((tm, d), row),
            pl.BlockSpec((tm, d), row),
            pl.BlockSpec((tm, d), row),
            pl.BlockSpec((tm, d), lambda i, f: (i, 5)),
            pl.BlockSpec((tm, d), lambda i, f: (i, 6)),
            pl.BlockSpec((tm, d), lambda i, f: (i, 7)),
            pl.BlockSpec((d, d), const),
            pl.BlockSpec((1, d), const),
            pl.BlockSpec((d, tf), lambda i, f: (0, f)),
            pl.BlockSpec((tf, d), lambda i, f: (f, 0)),
        ],
        out_specs=pl.BlockSpec((tm, d), row),
        out_shape=jax.ShapeDtypeStruct((n, d), F32),
        scratch_shapes=[pltpu.VMEM((tm, d), BF16), pltpu.VMEM((tm, d), F32)],
        compiler_params=pltpu.CompilerParams(
            dimension_semantics=("arbitrary", "arbitrary"), vmem_limit_bytes=VMEM_LIMIT_BYTES),
        name="mlp",
    )(x, oa, ob, p, p, p, wo, g2, wu, wd)


def _pick(n, target):
    t = min(n, target)
    while n % t:
        t //= 2
    return t


def kernel(x_prompt, x_sample, cache_k, cache_v, state_gla, page_table, norm1_g, w_in, q_norm_g, k_norm_g,
           sb_bias, w_alpha2, b_alpha, gla_norm_g, b_merge, w_out, norm2_g, w_up, w_down):
    bp, seq, d = x_prompt.shape
    bs, tq, _ = x_sample.shape
    depth, n_pool, page, n_heads, d_head = cache_k.shape
    _, _, g_heads, dk, dv = state_gla.shape
    rank = w_alpha2.shape[1]
    kw = g_heads * dk
    assert n_heads * d_head == d and g_heads * dv == d and 2 * kw == d and n_heads % 2 == 0
    assert 2 * d_head == LANES and rank <= LANES

    ck = jnp.transpose(cache_k, (0, 1, 3, 4, 2)).reshape(depth, n_pool, d, page)
    cv = jnp.transpose(cache_v, (0, 1, 3, 4, 2)).reshape(depth, n_pool, d, page)
    xp = x_prompt.reshape(bp * seq, d)
    xs = x_sample.reshape(bs * tq, d)

    gw = min(d, MXU_DIM)
    gi = jnp.arange(gw) // d_head
    gsum = (gi[:, None] == gi[None, :]).astype(BF16)
    ki = jnp.arange(MXU_DIM)
    tri = (ki[:, None] > ki[None, :]).astype(BF16)
    sb_tq = _pick(seq, SB_QUERY_BLOCK)
    n_grp = _pick(page_table.shape[1], SB_PAGES_PER_STEP)
    qks = jnp.concatenate([jnp.full((1, kw), dk ** -0.5, F32), jnp.ones((1, kw), F32)], axis=1)
    chunk_p, sub_p = 64, 16
    cm_p = _chunk_matrices(chunk_p, sub_p)
    cm_s = _chunk_matrices(tq, tq)
    s0_p = jnp.zeros((bp, g_heads, dk, dv), F32)

    o_qa, o_ka, o_va = 0, d, 2 * d
    o_qb, o_kb, o_vb, o_rb = 3 * d, 3 * d + kw, 3 * d + 2 * kw, 4 * d + 2 * kw
    o_alr = o_rb + d
    o_ga = o_alr + rank
    o_gb = o_ga + d

    yp, ys = xp, xs
    outs = [[] for _ in range(6)]
    for l in range(depth):
        wl = w_in[l]
        w_main = jnp.concatenate([wl[:, :o_alr], wl[:, o_ga:]], axis=1).astype(BF16)
        w_alr = jnp.pad(wl[:, o_alr:o_ga], ((0, 0), (0, LANES - rank))).astype(BF16)
        g1 = norm1_g[l].reshape(1, d)
        qg = jnp.tile(q_norm_g[l], n_heads).reshape(1, d)
        kg = jnp.tile(k_norm_g[l], n_heads).reshape(1, d)
        wa = jnp.pad(w_alpha2[l], ((0, LANES - rank), (0, 0)))
        ba = b_alpha[l].reshape(1, kw)
        gg = gla_norm_g[l].reshape(1, dv)
        bm = b_merge[l]
        wo = w_out[l].astype(BF16)
        g2 = norm2_g[l].reshape(1, d)
        wu = w_up[l].astype(BF16)
        wd = w_down[l].astype(BF16)
        bias_col = jnp.repeat(sb_bias[l], tq).reshape(n_heads * tq, 1)

        pp, kp, vp, alr_p = _proj(yp, g1, w_main, w_alr, qg, kg, gsum, qks, bm, tm=_pick(bp * seq, 512),
                                  d_head=d_head)
        oa_p = _sb_prompt(pp, sb_bias[l], tri, batch=bp, seq=seq, n_heads=n_heads, d_head=d_head, tq=sb_tq)
        ob_p, sp = _gla(pp, alr_p, wa, ba, s0_p, gg, cm_p, batch=bp, seq=seq, nseq=1, tb=_pick(seq, 512),
                        chunk=chunk_p, sub=sub_p, d_model=d)
        yp = _mlp(yp, oa_p, ob_p, pp, wo, g2, wu, wd, tm=_pick(bp * seq, 512), tf=_pick(w_up.shape[2], 1024))

        ps, ksm, vsm, alr_s = _proj(ys, g1, w_main, w_alr, qg, kg, gsum, qks, bm, tm=_pick(bs * tq, 256),
                                    d_head=d_head)
        q_s = ps[:, :d].astype(F32)
        oa_s = _sb_paged(q_s, ksm, vsm, ck, cv, page_table, bias_col, tri, layer=l, n_heads=n_heads,
                         d_head=d_head, tq=tq, n_grp=n_grp)
        ob_s, ss = _gla(ps, alr_s, wa, ba, state_gla[l], gg, cm_s, batch=bs, seq=tq, nseq=2, tb=tq,
                        chunk=tq, sub=tq, d_model=d)
        ys = _mlp(ys, oa_s, ob_s, ps, wo, g2, wu, wd, tm=_pick(bs * tq, 256), tf=_pick(w_up.shape[2], 1024))

        for lst, val in zip(outs, (kp, vp, sp, ksm, vsm, ss)):
            lst.append(val)

    k_prompt = jnp.stack(outs[0]).reshape(depth, bp, seq, n_heads, d_head)
    v_prompt = jnp.stack(outs[1]).reshape(depth, bp, seq, n_heads, d_head)
    gla_prompt = jnp.stack(outs[2])
    k_sample = jnp.stack(outs[3]).reshape(depth, bs, tq, n_heads, d_head)
    v_sample = jnp.stack(outs[4]).reshape(depth, bs, tq, n_heads, d_head)
    gla_sample = jnp.stack(outs[5])
    return (yp.reshape(bp, seq, d), ys.reshape(bs, tq, d), k_prompt, v_prompt, gla_prompt,
            k_sample, v_sample, gla_sample)
```

```python
import functools

import jax
import jax.numpy as jnp
from jax import lax
from jax.experimental import pallas as pl
from jax.experimental.pallas import tpu as pltpu

F32 = jnp.float32
BF16 = jnp.bfloat16

RMS_EPS = 1e-6
GLA_GATE_TAU = 16.0
LANES = 128
MXU_DIM = 256
VMEM_LIMIT_BYTES = 52 * 1024 * 1024
SB_QUERY_BLOCK = 1024
SB_PAGES_PER_STEP = 8


def _split_bf16(x):
    hi = x.astype(BF16)
    lo = (x - hi.astype(F32)).astype(BF16)
    return hi, lo


def _dot(a, b):
    return jnp.dot(a, b, preferred_element_type=F32)


def _dot_nt(a, b):
    return lax.dot_general(a, b, (((1,), (1,)), ((), ())), preferred_element_type=F32)


def _dot_tn(a, b):
    return lax.dot_general(a, b, (((0,), (0,)), ((), ())), preferred_element_type=F32)


LOG2E = 1.4426950408889634
SOFTPLUS2_CLAMP = 100.0


def _softplus2(z):
    return jnp.maximum(z, jnp.log(1.0 + jnp.exp2(jnp.minimum(z, SOFTPLUS2_CLAMP))) * LOG2E)


def _sigmoid(z):
    return 1.0 / (1.0 + jnp.exp(-z))


def _proj_kernel(x_ref, g1_ref, w_ref, wt_ref, walr_ref, qg_ref, kg_ref, kgt_ref, gsum_ref, qks_ref, bm_ref,
                 *refs, d_head, sb_scale, kv_t):
    if kv_t:
        p_ref, k_ref, v_ref, alr_ref, kvt_ref, xn_scr = refs[-6:]
    else:
        p_ref, k_ref, v_ref, alr_ref, xn_scr = refs
    j = pl.program_id(1)
    d_model = x_ref.shape[1]
    gw = gsum_ref.shape[0]

    @pl.when(j == 0)
    def _():
        x = x_ref[...]
        y = x * lax.rsqrt(jnp.mean(x * x, axis=-1, keepdims=True) + RMS_EPS) * g1_ref[...]
        xn = y.astype(BF16)
        xn_scr[...] = xn
        alr_ref[...] = _dot(xn, walr_ref[...])

    def proj():
        return _dot(xn_scr[...], w_ref[...])

    def head_norm(a, g):
        outs = []
        for c0 in range(0, d_model, gw):
            ac = a[:, c0:c0 + gw]
            hi, lo = _split_bf16(ac * ac)
            ss = _dot(hi, gsum_ref[...]) + _dot(lo, gsum_ref[...])
            outs.append(ac * lax.rsqrt(ss * (1.0 / d_head) + RMS_EPS))
        return jnp.concatenate(outs, axis=1) * g

    def proj_t():
        return _dot_nt(wt_ref[...], xn_scr[...])

    def head_norm_t(a, g_col):
        outs = []
        for r0 in range(0, d_model, d_head):
            ah = a[r0:r0 + d_head, :]
            ss = jnp.sum(ah * ah, axis=0, keepdims=True)
            outs.append(ah * lax.rsqrt(ss * (1.0 / d_head) + RMS_EPS))
        return jnp.concatenate(outs, axis=0) * g_col

    @pl.when(j == 0)
    def _():
        p_ref[...] = (head_norm(proj(), qg_ref[...]) * sb_scale).astype(BF16)

    @pl.when(j == 1)
    def _():
        if kv_t:
            kf = head_norm_t(proj_t(), kgt_ref[...])
            k_ref[...] = kf
            kvt_ref[...] = kf.astype(BF16)
        else:
            kf = head_norm(proj(), kg_ref[...])
            k_ref[...] = kf
            p_ref[...] = kf.astype(BF16)

    @pl.when(j == 2)
    def _():
        if kv_t:
            acc = proj_t()
            v_ref[...] = acc
            kvt_ref[...] = acc.astype(BF16)
        else:
            acc = proj()
            v_ref[...] = acc
            p_ref[...] = acc.astype(BF16)

    @pl.when(j == 3)
    def _():
        p_ref[...] = (proj() * qks_ref[...]).astype(BF16)

    @pl.when(j == 4)
    def _():
        p_ref[...] = proj().astype(BF16)

    @pl.when(j == 5)
    def _():
        acc = proj()
        p_ref[...] = (acc * _sigmoid(acc)).astype(BF16)

    @pl.when(j == 6)
    def _():
        p_ref[...] = _sigmoid(proj() + bm_ref[0:1, :]).astype(BF16)

    @pl.when(j == 7)
    def _():
        p_ref[...] = _sigmoid(proj() + bm_ref[1:2, :]).astype(BF16)


def _proj(x, g1, w, wt, walr, qg, kg, kgt, gsum, qks, bm, *, tm, d_head, kv_t=None):
    n, d = x.shape
    nblk = w.shape[1] // d
    const = lambda i, j: (0, 0)
    head = [pl.BlockSpec((tm, d), lambda i, j: (i, 0)), pl.BlockSpec((1, d), const)]
    tail = [
        pl.BlockSpec((d, LANES), const),
        pl.BlockSpec((1, d), const),
        pl.BlockSpec((1, d), const),
        pl.BlockSpec((d, 1), const),
        pl.BlockSpec(gsum.shape, const),
        pl.BlockSpec((1, d), const),
        pl.BlockSpec((2, d), const),
    ]
    params = pltpu.CompilerParams(dimension_semantics=("arbitrary", "arbitrary"),
                                  vmem_limit_bytes=VMEM_LIMIT_BYTES)
    kern = functools.partial(_proj_kernel, d_head=d_head, sb_scale=d_head ** -0.5 * LOG2E, kv_t=kv_t is not None)
    args = (x, g1, w, wt, walr, qg, kg, kgt, gsum, qks, bm)
    if kv_t is None:
        return pl.pallas_call(
            kern,
            grid=(n // tm, nblk),
            in_specs=head + [pl.BlockSpec((d, d), lambda i, j: (0, j)), pl.BlockSpec((d, d), const)] + tail,
            out_specs=[
                pl.BlockSpec((tm, d), lambda i, j: (i, j)),
                pl.BlockSpec((tm, d), lambda i, j: (i, 0)),
                pl.BlockSpec((tm, d), lambda i, j: (i, 0)),
                pl.BlockSpec((tm, LANES), lambda i, j: (i, 0)),
            ],
            out_shape=[
                jax.ShapeDtypeStruct((n, nblk * d), BF16),
                jax.ShapeDtypeStruct((n, d), F32),
                jax.ShapeDtypeStruct((n, d), F32),
                jax.ShapeDtypeStruct((n, LANES), F32),
            ],
            scratch_shapes=[pltpu.VMEM((tm, d), BF16)],
            compiler_params=params,
            name="proj",
        )(*args)

    layer, depth, batch, seq, prev_k, prev_v = kv_t
    nt = seq // tm
    w_col = lambda i, j: (0, jnp.where(j == 1, 0, jnp.where(j == 2, 3, j)))
    wt_row = lambda i, j: (jnp.clip(j - 1, 0, 1), 0)
    kv_blk = lambda i, j: (layer, i // nt, 0, i % nt)
    aliases, prev_args, prev_specs = {}, (), []
    if prev_k is not None:
        aliases = {len(args): 1, len(args) + 1: 2}
        prev_args = (prev_k, prev_v)
        prev_specs = [pl.BlockSpec(memory_space=pl.ANY)] * 2
    return pl.pallas_call(
        kern,
        grid=(n // tm, nblk),
        in_specs=head + [pl.BlockSpec((d, d), w_col), pl.BlockSpec((d, d), wt_row)] + tail + prev_specs,
        out_specs=[
            pl.BlockSpec((tm, d), lambda i, j: (i, jnp.maximum(j - 2, 0))),
            pl.BlockSpec((None, None, d, tm), kv_blk),
            pl.BlockSpec((None, None, d, tm), kv_blk),
            pl.BlockSpec((tm, LANES), lambda i, j: (i, 0)),
            pl.BlockSpec((None, d, tm), lambda i, j: (i // nt, jnp.clip(j - 1, 0, 1), i % nt)),
        ],
        out_shape=[
            jax.ShapeDtypeStruct((n, (nblk - 2) * d), BF16),
            jax.ShapeDtypeStruct((depth, batch, d, seq), F32),
            jax.ShapeDtypeStruct((depth, batch, d, seq), F32),
            jax.ShapeDtypeStruct((n, LANES), F32),
            jax.ShapeDtypeStruct((batch, 2 * d, seq), BF16),
        ],
        scratch_shapes=[pltpu.VMEM((tm, d), BF16)],
        input_output_aliases=aliases,
        compiler_params=params,
        name="proj_t",
    )(*args, *prev_args)


def _sb_prompt_kernel(bias_ref, q_ref, k_ref, v_ref, tri_ref, o_ref, acc_scr, c_scr, *, tq, tk, d_head):
    pair = pl.program_id(1)
    i = pl.program_id(2)
    n_sub = tq // tk
    lane = lax.broadcasted_iota(jnp.int32, (tq, 2 * d_head), 1)
    q = q_ref[...]
    qh = [jnp.where(lane < d_head, q, jnp.zeros_like(q)), jnp.where(lane >= d_head, q, jnp.zeros_like(q))]
    bias = [bias_ref[2 * pair] * LOG2E, bias_ref[2 * pair + 1] * LOG2E]

    acc_scr[...] = jnp.zeros_like(acc_scr)
    c_scr[...] = jnp.zeros_like(c_scr)

    def step(j0, r0, masked):
        kb = k_ref[:, pl.ds(j0, tk)]
        vb = v_ref[:, pl.ds(j0, tk)]
        if masked:
            visible = (lax.broadcasted_iota(jnp.int32, (tq - r0, tk), 1)
                       < lax.broadcasted_iota(jnp.int32, (tq - r0, tk), 0))
        for h in range(2):
            z = _dot(qh[h][r0:], kb) + bias[h]
            t = _softplus2(z)
            if masked:
                t = jnp.where(visible, t, 0.0)
            r = _dot(t.astype(BF16), tri_ref[...])
            c = c_scr[h, r0:, :]
            w = jnp.exp2(z - t - r - c)
            if masked:
                w = jnp.where(visible, w, 0.0)
            acc_scr[h, r0:, :] += _dot_nt(w.astype(BF16), vb)
            c_scr[h, r0:, :] = c + (r[:, 0:1] + t[:, 0:1])

    for dj in reversed(range(n_sub)):
        step(pl.multiple_of(i * tq + dj * tk, tk), dj * tk, True)

    def body(n, carry):
        step(pl.multiple_of((i * n_sub - 1 - n) * tk, tk), 0, False)
        return carry

    lax.fori_loop(0, i * n_sub, body, 0)
    o_ref[...] = jnp.where(lane < d_head, acc_scr[0], acc_scr[1]).astype(o_ref.dtype)


def _sb_prompt(p, kvt, bias, tri, *, batch, seq, n_heads, d_head, tq):
    n = batch * seq
    d = n_heads * d_head
    pw = 2 * d_head
    npair = n_heads // 2
    nq = seq // tq
    tk = tri.shape[0]
    return pl.pallas_call(
        functools.partial(_sb_prompt_kernel, tq=tq, tk=tk, d_head=d_head),
        grid_spec=pltpu.PrefetchScalarGridSpec(
            num_scalar_prefetch=1,
            grid=(batch, npair, nq),
            in_specs=[
                pl.BlockSpec((tq, pw), lambda b, h, i, s: (b * nq + i, h)),
                pl.BlockSpec((None, pw, seq), lambda b, h, i, s: (b, h, 0)),
                pl.BlockSpec((None, pw, seq), lambda b, h, i, s: (b, npair + h, 0)),
                pl.BlockSpec((tk, tk), lambda b, h, i, s: (0, 0)),
            ],
            out_specs=pl.BlockSpec((tq, pw), lambda b, h, i, s: (b * nq + i, h)),
            scratch_shapes=[pltpu.VMEM((2, tq, pw), F32), pltpu.VMEM((2, tq, 1), F32)],
        ),
        out_shape=jax.ShapeDtypeStruct((n, d), BF16),
        compiler_params=pltpu.CompilerParams(
            dimension_semantics=("arbitrary", "arbitrary", "arbitrary"), vmem_limit_bytes=VMEM_LIMIT_BYTES),
        name="sb_prompt",
    )(bias, p, kvt, kvt, tri)


def _sb_paged_kernel(pt_ref, q_ref, kn_ref, vn_ref, *refs, n_heads, d_head, tq, n_grp):
    kc_refs = refs[:n_grp]
    vc_refs = refs[n_grp:2 * n_grp]
    bc_ref, tri_ref, o_ref, acc_scr, c_scr, qm_scr = refs[2 * n_grp:]
    g = pl.program_id(1)
    nl = n_heads * tq
    d = n_heads * d_head
    tw = tri_ref.shape[0]
    row_head = lax.broadcasted_iota(jnp.int32, (nl, d), 0) // tq
    col_head = lax.broadcasted_iota(jnp.int32, (nl, d), 1) // d_head
    same_head = row_head == col_head

    def weights(z, mask):
        t = _softplus2(z)
        if mask is not None:
            t = jnp.where(mask, t, 0.0)
        c = c_scr[...]
        n_chunks = z.shape[1] // tw
        ws = [None] * n_chunks
        for ci in reversed(range(n_chunks)):
            cols = slice(ci * tw, (ci + 1) * tw)
            tc = t[:, cols]
            r = _dot(tc.astype(BF16), tri_ref[...])
            w = jnp.exp2(z[:, cols] - tc - r - c)
            if mask is not None:
                w = jnp.where(mask[:, cols], w, 0.0)
            ws[ci] = w.astype(BF16)
            c = c + jnp.sum(tc, axis=1, keepdims=True)
        c_scr[...] = c
        return ws[0] if n_chunks == 1 else jnp.concatenate(ws, axis=1)

    @pl.when(g == 0)
    def _():
        q = q_ref[...]
        qm = jnp.where(same_head, jnp.concatenate([q] * n_heads, axis=0), 0.0).astype(BF16)
        qm_scr[...] = qm
        c_scr[...] = jnp.zeros_like(c_scr)
        pad = jnp.zeros((tw - tq, d), F32)
        kn = jnp.concatenate([kn_ref[...], pad], axis=0).astype(BF16)
        vn = jnp.concatenate([vn_ref[...], pad], axis=0).astype(BF16)
        s_idx = lax.broadcasted_iota(jnp.int32, (nl, tw), 1)
        t_idx = lax.broadcasted_iota(jnp.int32, (nl, tw), 0) % tq
        w = weights(_dot_nt(qm, kn) + bc_ref[...] * LOG2E, s_idx < t_idx)
        acc_scr[...] = _dot(w, vn)

    kt = jnp.concatenate([r[...].astype(BF16) for r in kc_refs], axis=1)
    w = weights(_dot(qm_scr[...], kt) + bc_ref[...] * LOG2E, None)
    vt = jnp.concatenate([r[...].astype(BF16) for r in vc_refs], axis=1)
    acc_scr[...] += _dot_nt(w, vt)

    @pl.when(g == pl.num_programs(1) - 1)
    def _():
        a = jnp.where(same_head, acc_scr[...], 0.0)
        out = a[0:tq, :]
        for h in range(1, n_heads):
            out = out + a[h * tq:(h + 1) * tq, :]
        o_ref[...] = out.astype(o_ref.dtype)


def _sb_paged(q, k_new, v_new, cache_kt, cache_vt, page_table, bias_col, tri, *, layer, n_heads, d_head, tq,
              n_grp):
    n, d = q.shape
    bs, n_pages = page_table.shape
    page = cache_kt.shape[3]
    nl = n_heads * tq
    assert (n_grp * page) % tri.shape[0] == 0 and n_pages % n_grp == 0
    tok = lambda b, g, pt: (b, 0)
    const = lambda b, g, pt: (0, 0)

    def cache(j):
        return pl.BlockSpec((None, None, d, page),
                            lambda b, g, pt: (layer, pt[b, n_pages - n_grp * (g + 1) + j], 0, 0))

    return pl.pallas_call(
        functools.partial(_sb_paged_kernel, n_heads=n_heads, d_head=d_head, tq=tq, n_grp=n_grp),
        grid_spec=pltpu.PrefetchScalarGridSpec(
            num_scalar_prefetch=1,
            grid=(bs, n_pages // n_grp),
            in_specs=([pl.BlockSpec((tq, d), tok)] * 3
                      + [cache(j) for j in range(n_grp)] * 2
                      + [pl.BlockSpec((nl, 1), const), pl.BlockSpec(tri.shape, const)]),
            out_specs=pl.BlockSpec((tq, d), tok),
            scratch_shapes=[pltpu.VMEM((nl, d), F32), pltpu.VMEM((nl, 1), F32), pltpu.VMEM((nl, d), BF16)],
        ),
        out_shape=jax.ShapeDtypeStruct((n, d), F32),
        compiler_params=pltpu.CompilerParams(
            dimension_semantics=("arbitrary", "arbitrary"), vmem_limit_bytes=VMEM_LIMIT_BYTES),
        name="sb_paged",
    )(page_table, q, k_new, v_new, *([cache_kt] * n_grp), *([cache_vt] * n_grp), bias_col, tri)


def _gla_kernel(qk_ref, v_ref, alr_ref, wa_ref, ba_ref, s0_ref, g_ref, cm_ref, o_ref, sfin_ref, s_scr,
                *, nseq, tb, chunk, sub, n_heads, dk, dv):
    t_blk = pl.program_id(1)
    kw = n_heads * dk
    n_sub = chunk // sub

    @pl.when(t_blk == 0)
    def _():
        s_scr[...] = s0_ref[...]

    ti = lax.broadcasted_iota(jnp.int32, (chunk, chunk), 0)
    si = lax.broadcasted_iota(jnp.int32, (chunk, chunk), 1)
    lag = ti // sub - si // sub
    diag_mask = (lag == 0) & (si <= ti)
    eye = (lax.broadcasted_iota(jnp.int32, (dk, dk), 0) == lax.broadcasted_iota(jnp.int32, (dk, dk), 1))

    def one_chunk(seq, r0):
        rows = pl.ds(r0, chunk)
        a_hi, a_lo = _split_bf16(alr_ref[rows, :])
        w_hi, w_lo = _split_bf16(wa_ref[...])
        pre = _dot(a_hi, w_hi) + _dot(a_lo, w_hi) + _dot(a_hi, w_lo) + ba_ref[...]
        log_a = (jnp.minimum(pre, 0.0) - jnp.log(1.0 + jnp.exp(-jnp.abs(pre)))) * (1.0 / GLA_GATE_TAU)
        l_hi, l_lo = _split_bf16(log_a)
        stats = _dot(cm_ref[...], l_hi) + _dot(cm_ref[...], l_lo)
        b_all = stats[0 * chunk:1 * chunk]
        rs_all = stats[1 * chunk:2 * chunk]
        re_all = stats[2 * chunk:3 * chunk]
        mid_all = stats[3 * chunk:4 * chunk]
        last_all = stats[4 * chunk:5 * chunk]
        qk = qk_ref[rows, :].astype(F32)
        vv = v_ref[rows, :]
        for h in range(n_heads):
            ks = slice(h * dk, (h + 1) * dk)
            q = qk[:, h * dk:(h + 1) * dk]
            k = qk[:, kw + h * dk:kw + (h + 1) * dk]
            v = vv[:, h * dv:(h + 1) * dv]
            b, rs, re, mid, last = b_all[:, ks], rs_all[:, ks], re_all[:, ks], mid_all[:, ks], last_all[:, ks]
            q_dec = q * jnp.exp(b - rs)
            k_dec = (k * jnp.exp(re - b)).astype(BF16)
            sc = jnp.where(diag_mask,
                           _dot_nt((q * jnp.exp(b - mid)).astype(BF16), (k * jnp.exp(mid - b)).astype(BF16)), 0.0)
            for d_lag in range(1, n_sub):
                ql = q_dec
                if d_lag > 1:
                    shift = sub * (d_lag - 1)
                    rs_prev = jnp.concatenate([jnp.zeros((shift, dk), F32), rs[:chunk - shift]], axis=0)
                    ql = q_dec * jnp.exp(rs - rs_prev)
                sc = sc + jnp.where(lag == d_lag, _dot_nt(ql.astype(BF16), k_dec), 0.0)
            s_old = s_scr[seq, h]
            o = _dot((q_dec * jnp.exp(rs)).astype(BF16), s_old.astype(BF16)) + _dot(sc.astype(BF16), v)
            k_st = (k * jnp.exp(last - b)).astype(BF16)
            decay_col = jnp.sum(jnp.where(eye, jnp.exp(last[0:1, :]), 0.0), axis=1, keepdims=True)
            s_scr[seq, h] = s_old * decay_col + _dot_tn(k_st, v)
            on = o * lax.rsqrt(jnp.mean(o * o, axis=-1, keepdims=True) + RMS_EPS) * g_ref[...]
            o_ref[rows, h * dv:(h + 1) * dv] = on.astype(o_ref.dtype)

    n_chunks = tb // chunk
    for seq in range(nseq):
        if n_chunks == 1:
            one_chunk(seq, seq * tb)
        else:
            unroll = 2 if n_chunks % 2 == 0 else 1

            def body(n, carry, seq=seq, unroll=unroll):
                for u in range(unroll):
                    one_chunk(seq, pl.multiple_of(seq * tb + (n * unroll + u) * chunk, chunk))
                return carry
            lax.fori_loop(0, n_chunks // unroll, body, 0)

    @pl.when(t_blk == pl.num_programs(1) - 1)
    def _():
        sfin_ref[...] = s_scr[...]


def _gla(p, alr, wa, ba, s0, g, cm, *, batch, seq, nseq, tb, chunk, sub, col0):
    n = batch * seq
    _, n_heads, dk, dv = s0.shape
    kw = n_heads * dk
    vw = n_heads * dv
    nt = seq // tb
    rows = nseq * tb
    tok = lambda b, t: (b * nt + t, 0)
    const = lambda b, t: (0, 0)
    return pl.pallas_call(
        functools.partial(_gla_kernel, nseq=nseq, tb=tb, chunk=chunk, sub=sub, n_heads=n_heads, dk=dk, dv=dv),
        grid=(batch // nseq, nt),
        in_specs=[
            pl.BlockSpec((rows, 2 * kw), lambda b, t: (b * nt + t, col0)),
            pl.BlockSpec((rows, vw), lambda b, t: (b * nt + t, col0 + 1)),
            pl.BlockSpec((rows, LANES), tok),
            pl.BlockSpec((LANES, kw), const),
            pl.BlockSpec((1, kw), const),
            pl.BlockSpec((nseq, n_heads, dk, dv), lambda b, t: (b, 0, 0, 0)),
            pl.BlockSpec((1, dv), const),
            pl.BlockSpec(cm.shape, const),
        ],
        out_specs=[
            pl.BlockSpec((rows, vw), tok),
            pl.BlockSpec((nseq, n_heads, dk, dv), lambda b, t: (b, 0, 0, 0)),
        ],
        out_shape=[
            jax.ShapeDtypeStruct((n, vw), BF16),
            jax.ShapeDtypeStruct(s0.shape, F32),
        ],
        scratch_shapes=[pltpu.VMEM((nseq, n_heads, dk, dv), F32)],
        compiler_params=pltpu.CompilerParams(
            dimension_semantics=("arbitrary", "arbitrary"), vmem_limit_bytes=VMEM_LIMIT_BYTES),
        name="gla",
    )(p, p, alr, wa, ba, s0, g, cm)


def _chunk_matrices(chunk, sub):
    t = jnp.arange(chunk)[:, None]
    j = jnp.arange(chunk)[None, :]
    s0 = (t // sub) * sub
    mats = [j <= t, j < s0, j < s0 + sub, j < s0 + sub // 2, j < chunk + 0 * t]
    return jnp.concatenate(mats, axis=0).astype(BF16)


def _mlp_kernel(x_ref, oa_ref, ob_ref, rb_ref, ga_ref, gb_ref, wo_ref, g2_ref, wu_ref, wd_ref, y_ref,
                hn_scr, acc_scr):
    f = pl.program_id(1)

    @pl.when(f == 0)
    def _():
        mixed = (ga_ref[...].astype(F32) * oa_ref[...].astype(F32)
                 + gb_ref[...].astype(F32) * (ob_ref[...].astype(F32) * rb_ref[...].astype(F32)))
        h = x_ref[...] + _dot(mixed.astype(BF16), wo_ref[...])
        acc_scr[...] = h
        hn = h * lax.rsqrt(jnp.mean(h * h, axis=-1, keepdims=True) + RMS_EPS) * g2_ref[...]
        hn_scr[...] = hn.astype(BF16)

    u = jnp.maximum(_dot(hn_scr[...], wu_ref[...]), 0.0)
    acc_scr[...] += _dot((u * u).astype(BF16), wd_ref[...])

    @pl.when(f == pl.num_programs(1) - 1)
    def _():
        y_ref[...] = acc_scr[...]


def _mlp(x, oa, ob, p, wo, g2, wu, wd, *, tm, tf, col0):
    n, d = x.shape
    dff = wu.shape[1]
    row = lambda i, f: (i, 0)
    const = lambda i, f: (0, 0)
    return pl.pallas_call(
        _mlp_kernel,
        grid=(n // tm, dff // tf),
        in_specs=[
            pl.BlockSpec((tm, d), row),
            pl.BlockSpec((tm, d), row),
            pl.BlockSpec((tm, d), row),
            pl.BlockSpec((tm, d), lambda i, f: (i, col0)),
            pl.BlockSpec((tm, d), lambda i, f: (i, col0 + 1)),
            pl.BlockSpec((tm, d), lambda i, f: (i, col0 + 2)),
            pl.BlockSpec((d, d), const),
            pl.BlockSpec((1, d), const),
            pl.BlockSpec((d, tf), lambda i, f: (0, f)),
            pl.BlockSpec((tf, d), lambda i, f: (f, 0)),
        ],
        out_specs=pl.BlockSpec((tm, d), row),
        out_shape=jax.ShapeDtypeStruct((n, d), F32),
        scratch_shapes=[pltpu.VMEM((tm, d), BF16), pltpu.VMEM((tm, d), F32)],
        compiler_params=pltpu.CompilerParams(
            dimension_semantics=("arbitrary", "arbitrary"), vmem_limit_bytes=VMEM_LIMIT_BYTES),
        name="mlp",
    )(x, oa, ob, p, p, p, wo, g2, wu, wd)


def _pick(n, target):
    t = min(n, target)
    while n % t:
        t //= 2
    return t


def kernel(x_prompt, x_sample, cache_k, cache_v, state_gla, page_table, norm1_g, w_in, q_norm_g, k_norm_g,
           sb_bias, w_alpha2, b_alpha, gla_norm_g, b_merge, w_out, norm2_g, w_up, w_down):
    bp, seq, d = x_prompt.shape
    bs, tq, _ = x_sample.shape
    depth, n_pool, page, n_heads, d_head = cache_k.shape
    _, _, g_heads, dk, dv = state_gla.shape
    rank = w_alpha2.shape[1]
    kw = g_heads * dk
    assert n_heads * d_head == d and g_heads * dv == d and 2 * kw == d and n_heads % 2 == 0
    assert 2 * d_head == LANES and rank <= LANES

    ck = jnp.transpose(cache_k, (0, 1, 3, 4, 2)).reshape(depth, n_pool, d, page)
    cv = jnp.transpose(cache_v, (0, 1, 3, 4, 2)).reshape(depth, n_pool, d, page)
    xp = x_prompt.reshape(bp * seq, d)
    xs = x_sample.reshape(bs * tq, d)

    gw = min(d, MXU_DIM)
    gi = jnp.arange(gw) // d_head
    gsum = (gi[:, None] == gi[None, :]).astype(BF16)
    ki = jnp.arange(MXU_DIM)
    tri = (ki[:, None] > ki[None, :]).astype(BF16)
    sb_tq = _pick(seq, SB_QUERY_BLOCK)
    n_grp = _pick(page_table.shape[1], SB_PAGES_PER_STEP)
    qks = jnp.concatenate([jnp.full((1, kw), dk ** -0.5, F32), jnp.ones((1, kw), F32)], axis=1)
    chunk_p, sub_p = 64, 16
    cm_p = _chunk_matrices(chunk_p, sub_p)
    cm_s = _chunk_matrices(tq, tq)
    s0_p = jnp.zeros((bp, g_heads, dk, dv), F32)

    o_qa, o_ka, o_va = 0, d, 2 * d
    o_qb, o_kb, o_vb, o_rb = 3 * d, 3 * d + kw, 3 * d + 2 * kw, 4 * d + 2 * kw
    o_alr = o_rb + d
    o_ga = o_alr + rank
    o_gb = o_ga + d

    yp, ys = xp, xs
    kp_t = vp_t = None
    outs = [[] for _ in range(4)]
    for l in range(depth):
        wl = w_in[l]
        w_main = jnp.concatenate([wl[:, :o_alr], wl[:, o_ga:]], axis=1).astype(BF16)
        w_alr = jnp.pad(wl[:, o_alr:o_ga], ((0, 0), (0, LANES - rank))).astype(BF16)
        g1 = norm1_g[l].reshape(1, d)
        qg = jnp.tile(q_norm_g[l], n_heads).reshape(1, d)
        kg = jnp.tile(k_norm_g[l], n_heads).reshape(1, d)
        wa = jnp.pad(w_alpha2[l], ((0, LANES - rank), (0, 0)))
        ba = b_alpha[l].reshape(1, kw)
        gg = gla_norm_g[l].reshape(1, dv)
        bm = b_merge[l]
        wo = w_out[l].astype(BF16)
        g2 = norm2_g[l].reshape(1, d)
        wu = w_up[l].astype(BF16)
        wd = w_down[l].astype(BF16)
        bias_col = jnp.repeat(sb_bias[l], tq).reshape(n_heads * tq, 1)

        wt_kv = w_main[:, d:3 * d].T
        kgt = kg.reshape(d, 1)

        pp, kp_t, vp_t, alr_p, kvt = _proj(yp, g1, w_main, wt_kv, w_alr, qg, kg, kgt, gsum, qks, bm,
                                           tm=_pick(seq, 1024), d_head=d_head,
                                           kv_t=(l, depth, bp, seq, kp_t, vp_t))
        oa_p = _sb_prompt(pp, kvt, sb_bias[l], tri, batch=bp, seq=seq, n_heads=n_heads, d_head=d_head, tq=sb_tq)
        ob_p, sp = _gla(pp, alr_p, wa, ba, s0_p, gg, cm_p, batch=bp, seq=seq, nseq=1, tb=_pick(seq, 512),
                        chunk=chunk_p, sub=sub_p, col0=1)
        yp = _mlp(yp, oa_p, ob_p, pp, wo, g2, wu, wd, tm=_pick(bp * seq, 512), tf=_pick(w_up.shape[2], 1024),
                  col0=3)

        ps, ksm, vsm, alr_s = _proj(ys, g1, w_main, wt_kv, w_alr, qg, kg, kgt, gsum, qks, bm,
                                    tm=_pick(bs * tq, 256), d_head=d_head)
        q_s = ps[:, :d].astype(F32)
        oa_s = _sb_paged(q_s, ksm, vsm, ck, cv, page_table, bias_col, tri, layer=l, n_heads=n_heads,
                         d_head=d_head, tq=tq, n_grp=n_grp)
        ob_s, ss = _gla(ps, alr_s, wa, ba, state_gla[l], gg, cm_s, batch=bs, seq=tq, nseq=2, tb=tq,
                        chunk=tq, sub=tq, col0=3)
        ys = _mlp(ys, oa_s, ob_s, ps, wo, g2, wu, wd, tm=_pick(bs * tq, 256), tf=_pick(w_up.shape[2], 1024),
                  col0=5)

        for lst, val in zip(outs, (sp, ksm, vsm, ss)):
            lst.append(val)

    k_prompt = jnp.transpose(kp_t.reshape(depth, bp, n_heads, d_head, seq), (0, 1, 4, 2, 3))
    v_prompt = jnp.transpose(vp_t.reshape(depth, bp, n_heads, d_head, seq), (0, 1, 4, 2, 3))
    gla_prompt = jnp.stack(outs[0])
    k_sample = jnp.stack(outs[1]).reshape(depth, bs, tq, n_heads, d_head)
    v_sample = jnp.stack(outs[2]).reshape(depth, bs, tq, n_heads, d_head)
    gla_sample = jnp.stack(outs[3])
    return (yp.reshape(bp, seq, d), ys.reshape(bs, tq, d), k_prompt, v_prompt, gla_prompt,
            k_sample, v_sample, gla_sample)
```

```python
import functools

import jax
import jax.numpy as jnp
from jax import lax
from jax.experimental import pallas as pl
from jax.experimental.pallas import tpu as pltpu

F32 = jnp.float32
BF16 = jnp.bfloat16

RMS_EPS = 1e-6
GLA_GATE_TAU = 16.0
LANES = 128
MXU_DIM = 256
VMEM_LIMIT_BYTES = 52 * 1024 * 1024
SB_QUERY_BLOCK = 1024
SB_PAGES_PER_STEP = 8


def _split_bf16(x):
    hi = x.astype(BF16)
    lo = (x - hi.astype(F32)).astype(BF16)
    return hi, lo


def _dot(a, b):
    return jnp.dot(a, b, preferred_element_type=F32)


def _dot_nt(a, b):
    return lax.dot_general(a, b, (((1,), (1,)), ((), ())), preferred_element_type=F32)


def _dot_tn(a, b):
    return lax.dot_general(a, b, (((0,), (0,)), ((), ())), preferred_element_type=F32)


LOG2E = 1.4426950408889634
SOFTPLUS2_CLAMP = 100.0


def _softplus2(z):
    return jnp.maximum(z, jnp.log(1.0 + jnp.exp2(jnp.minimum(z, SOFTPLUS2_CLAMP))) * LOG2E)


def _sigmoid(z):
    return 1.0 / (1.0 + jnp.exp(-z))


def _proj_kernel(x_ref, g1_ref, w_ref, wt_ref, walr_ref, qg_ref, kg_ref, kgt_ref, gsum_ref, qks_ref, bm_ref,
                 *refs, d_head, sb_scale, kv_t):
    if kv_t:
        p_ref, k_ref, v_ref, alr_ref, kvt_ref, xn_scr = refs[-6:]
    else:
        p_ref, k_ref, v_ref, alr_ref, xn_scr = refs
    j = pl.program_id(1)
    d_model = x_ref.shape[1]
    gw = gsum_ref.shape[0]

    @pl.when(j == 0)
    def _():
        x = x_ref[...]
        y = x * lax.rsqrt(jnp.mean(x * x, axis=-1, keepdims=True) + RMS_EPS) * g1_ref[...]
        xn = y.astype(BF16)
        xn_scr[...] = xn
        alr_ref[...] = _dot(xn, walr_ref[...])

    def proj():
        return _dot(xn_scr[...], w_ref[...])

    def head_norm(a, g):
        outs = []
        for c0 in range(0, d_model, gw):
            ac = a[:, c0:c0 + gw]
            hi, lo = _split_bf16(ac * ac)
            ss = _dot(hi, gsum_ref[...]) + _dot(lo, gsum_ref[...])
            outs.append(ac * lax.rsqrt(ss * (1.0 / d_head) + RMS_EPS))
        return jnp.concatenate(outs, axis=1) * g

    def proj_t():
        return _dot_nt(wt_ref[...], xn_scr[...])

    def head_norm_t(a, g_col):
        outs = []
        for r0 in range(0, d_model, d_head):
            ah = a[r0:r0 + d_head, :]
            ss = jnp.sum(ah * ah, axis=0, keepdims=True)
            outs.append(ah * lax.rsqrt(ss * (1.0 / d_head) + RMS_EPS))
        return jnp.concatenate(outs, axis=0) * g_col

    @pl.when(j == 0)
    def _():
        p_ref[...] = (head_norm(proj(), qg_ref[...]) * sb_scale).astype(BF16)

    @pl.when(j == 1)
    def _():
        if kv_t:
            kf = head_norm_t(proj_t(), kgt_ref[...])
            k_ref[...] = kf
            kvt_ref[...] = kf.astype(BF16)
        else:
            kf = head_norm(proj(), kg_ref[...])
            k_ref[...] = kf
            p_ref[...] = kf.astype(BF16)

    @pl.when(j == 2)
    def _():
        if kv_t:
            acc = proj_t()
            v_ref[...] = acc
            kvt_ref[...] = acc.astype(BF16)
        else:
            acc = proj()
            v_ref[...] = acc
            p_ref[...] = acc.astype(BF16)

    @pl.when(j == 3)
    def _():
        p_ref[...] = (proj() * qks_ref[...]).astype(BF16)

    @pl.when(j == 4)
    def _():
        p_ref[...] = proj().astype(BF16)

    @pl.when(j == 5)
    def _():
        acc = proj()
        p_ref[...] = (acc * _sigmoid(acc)).astype(BF16)

    @pl.when(j == 6)
    def _():
        p_ref[...] = _sigmoid(proj() + bm_ref[0:1, :]).astype(BF16)

    @pl.when(j == 7)
    def _():
        p_ref[...] = _sigmoid(proj() + bm_ref[1:2, :]).astype(BF16)


def _proj(x, g1, w, wt, walr, qg, kg, kgt, gsum, qks, bm, *, tm, d_head, kv_t=None):
    n, d = x.shape
    nblk = w.shape[1] // d
    const = lambda i, j: (0, 0)
    head = [pl.BlockSpec((tm, d), lambda i, j: (i, 0)), pl.BlockSpec((1, d), const)]
    tail = [
        pl.BlockSpec((d, LANES), const),
        pl.BlockSpec((1, d), const),
        pl.BlockSpec((1, d), const),
        pl.BlockSpec((d, 1), const),
        pl.BlockSpec(gsum.shape, const),
        pl.BlockSpec((1, d), const),
        pl.BlockSpec((2, d), const),
    ]
    params = pltpu.CompilerParams(dimension_semantics=("arbitrary", "arbitrary"),
                                  vmem_limit_bytes=VMEM_LIMIT_BYTES)
    kern = functools.partial(_proj_kernel, d_head=d_head, sb_scale=d_head ** -0.5 * LOG2E, kv_t=kv_t is not None)
    args = (x, g1, w, wt, walr, qg, kg, kgt, gsum, qks, bm)
    if kv_t is None:
        return pl.pallas_call(
            kern,
            grid=(n // tm, nblk),
            in_specs=head + [pl.BlockSpec((d, d), lambda i, j: (0, j)), pl.BlockSpec((d, d), const)] + tail,
            out_specs=[
                pl.BlockSpec((tm, d), lambda i, j: (i, j)),
                pl.BlockSpec((tm, d), lambda i, j: (i, 0)),
                pl.BlockSpec((tm, d), lambda i, j: (i, 0)),
                pl.BlockSpec((tm, LANES), lambda i, j: (i, 0)),
            ],
            out_shape=[
                jax.ShapeDtypeStruct((n, nblk * d), BF16),
                jax.ShapeDtypeStruct((n, d), F32),
                jax.ShapeDtypeStruct((n, d), F32),
                jax.ShapeDtypeStruct((n, LANES), F32),
            ],
            scratch_shapes=[pltpu.VMEM((tm, d), BF16)],
            compiler_params=params,
            name="proj",
        )(*args)

    layer, depth, batch, seq, prev_k, prev_v = kv_t
    nt = seq // tm
    w_col = lambda i, j: (0, jnp.where(j == 1, 0, jnp.where(j == 2, 3, j)))
    wt_row = lambda i, j: (jnp.clip(j - 1, 0, 1), 0)
    kv_blk = lambda i, j: (layer, i // nt, 0, i % nt)
    aliases, prev_args, prev_specs = {}, (), []
    if prev_k is not None:
        aliases = {len(args): 1, len(args) + 1: 2}
        prev_args = (prev_k, prev_v)
        prev_specs = [pl.BlockSpec(memory_space=pl.ANY)] * 2
    return pl.pallas_call(
        kern,
        grid=(n // tm, nblk),
        in_specs=head + [pl.BlockSpec((d, d), w_col), pl.BlockSpec((d, d), wt_row)] + tail + prev_specs,
        out_specs=[
            pl.BlockSpec((tm, d), lambda i, j: (i, jnp.maximum(j - 2, 0))),
            pl.BlockSpec((None, None, d, tm), kv_blk),
            pl.BlockSpec((None, None, d, tm), kv_blk),
            pl.BlockSpec((tm, LANES), lambda i, j: (i, 0)),
            pl.BlockSpec((None, d, tm), lambda i, j: (i // nt, jnp.clip(j - 1, 0, 1), i % nt)),
        ],
        out_shape=[
            jax.ShapeDtypeStruct((n, (nblk - 2) * d), BF16),
            jax.ShapeDtypeStruct((depth, batch, d, seq), F32),
            jax.ShapeDtypeStruct((depth, batch, d, seq), F32),
            jax.ShapeDtypeStruct((n, LANES), F32),
            jax.ShapeDtypeStruct((batch, 2 * d, seq), BF16),
        ],
        scratch_shapes=[pltpu.VMEM((tm, d), BF16)],
        input_output_aliases=aliases,
        compiler_params=params,
        name="proj_t",
    )(*args, *prev_args)


def _sb_prompt_kernel(bias_ref, q_ref, k_ref, v_ref, tri_ref, o_ref, acc_scr, c_scr, *, tq, tk, d_head):
    pair = pl.program_id(1)
    i = pl.program_id(2)
    n_sub = tq // tk
    pw = 2 * d_head
    lane = lax.broadcasted_iota(jnp.int32, (tq, pw), 1)
    krow = lax.broadcasted_iota(jnp.int32, (pw, tk), 0)
    q = q_ref[...]
    qf = q.astype(F32)
    qh, bias_rows, keep_rows = [], [], []
    for h in range(2):
        own = (lane < d_head) if h == 0 else (lane >= d_head)
        b0 = d_head if h == 0 else 0
        qh.append(jnp.where(own, qf, jnp.where(lane == b0, 1.0, jnp.where(lane == b0 + 1, 1.0, 0.0))).astype(BF16))
        b_all = jnp.full((pw, tk), bias_ref[2 * pair + h] * LOG2E, F32)
        b_hi = b_all.astype(BF16).astype(F32)
        bias_rows.append(jnp.where(krow == b0, b_hi, jnp.where(krow == b0 + 1, b_all - b_hi, 0.0)).astype(BF16))
        keep_rows.append(jnp.where(krow == b0, 0.0, jnp.where(krow == b0 + 1, 0.0, 1.0)).astype(BF16))

    acc_scr[...] = jnp.zeros_like(acc_scr)
    c_scr[...] = jnp.zeros_like(c_scr)

    def step(j0, r0, masked):
        kb = k_ref[:, pl.ds(j0, tk)]
        vb = v_ref[:, pl.ds(j0, tk)]
        if masked:
            visible = (lax.broadcasted_iota(jnp.int32, (tq - r0, tk), 1)
                       < lax.broadcasted_iota(jnp.int32, (tq - r0, tk), 0))
        for h in range(2):
            z = _dot(qh[h][r0:], kb * keep_rows[h] + bias_rows[h])
            t = _softplus2(z)
            if masked:
                t = jnp.where(visible, t, 0.0)
            r = _dot(t.astype(BF16), tri_ref[...])
            c = c_scr[h, r0:, :]
            w = jnp.exp2(z - t - r - c)
            if masked:
                w = jnp.where(visible, w, 0.0)
            acc_scr[h, r0:, :] += _dot_nt(w.astype(BF16), vb)
            c_scr[h, r0:, :] = c + (r[:, 0:1] + t[:, 0:1])

    for dj in reversed(range(n_sub)):
        step(pl.multiple_of(i * tq + dj * tk, tk), dj * tk, True)

    def body(n, carry):
        step(pl.multiple_of((i * n_sub - 1 - n) * tk, tk), 0, False)
        return carry

    lax.fori_loop(0, i * n_sub, body, 0)
    o_ref[...] = jnp.where(lane < d_head, acc_scr[0], acc_scr[1]).astype(o_ref.dtype)


def _sb_prompt(p, kvt, bias, tri, *, batch, seq, n_heads, d_head, tq):
    n = batch * seq
    d = n_heads * d_head
    pw = 2 * d_head
    npair = n_heads // 2
    nq = seq // tq
    tk = tri.shape[0]
    return pl.pallas_call(
        functools.partial(_sb_prompt_kernel, tq=tq, tk=tk, d_head=d_head),
        grid_spec=pltpu.PrefetchScalarGridSpec(
            num_scalar_prefetch=1,
            grid=(batch, npair, nq),
            in_specs=[
                pl.BlockSpec((tq, pw), lambda b, h, i, s: (b * nq + i, h)),
                pl.BlockSpec((None, pw, seq), lambda b, h, i, s: (b, h, 0)),
                pl.BlockSpec((None, pw, seq), lambda b, h, i, s: (b, npair + h, 0)),
                pl.BlockSpec((tk, tk), lambda b, h, i, s: (0, 0)),
            ],
            out_specs=pl.BlockSpec((tq, pw), lambda b, h, i, s: (b * nq + i, h)),
            scratch_shapes=[pltpu.VMEM((2, tq, pw), F32), pltpu.VMEM((2, tq, 1), F32)],
        ),
        out_shape=jax.ShapeDtypeStruct((n, d), BF16),
        compiler_params=pltpu.CompilerParams(
            dimension_semantics=("arbitrary", "arbitrary", "arbitrary"), vmem_limit_bytes=VMEM_LIMIT_BYTES),
        name="sb_prompt",
    )(bias, p, kvt, kvt, tri)


def _sb_paged_kernel(pt_ref, q_ref, kn_ref, vn_ref, *refs, n_heads, d_head, tq, n_grp):
    kc_refs = refs[:n_grp]
    vc_refs = refs[n_grp:2 * n_grp]
    bc_ref, tri_ref, o_ref, acc_scr, c_scr, qm_scr = refs[2 * n_grp:]
    g = pl.program_id(1)
    nl = n_heads * tq
    d = n_heads * d_head
    tw = tri_ref.shape[0]
    row_head = lax.broadcasted_iota(jnp.int32, (nl, d), 0) // tq
    col_head = lax.broadcasted_iota(jnp.int32, (nl, d), 1) // d_head
    same_head = row_head == col_head

    def weights(z, mask):
        t = _softplus2(z)
        if mask is not None:
            t = jnp.where(mask, t, 0.0)
        c = c_scr[...]
        n_chunks = z.shape[1] // tw
        ws = [None] * n_chunks
        for ci in reversed(range(n_chunks)):
            cols = slice(ci * tw, (ci + 1) * tw)
            tc = t[:, cols]
            r = _dot(tc.astype(BF16), tri_ref[...])
            w = jnp.exp2(z[:, cols] - tc - r - c)
            if mask is not None:
                w = jnp.where(mask[:, cols], w, 0.0)
            ws[ci] = w.astype(BF16)
            c = c + jnp.sum(tc, axis=1, keepdims=True)
        c_scr[...] = c
        return ws[0] if n_chunks == 1 else jnp.concatenate(ws, axis=1)

    @pl.when(g == 0)
    def _():
        q = q_ref[...]
        qm = jnp.where(same_head, jnp.concatenate([q] * n_heads, axis=0), 0.0).astype(BF16)
        qm_scr[...] = qm
        c_scr[...] = jnp.zeros_like(c_scr)
        pad = jnp.zeros((tw - tq, d), F32)
        kn = jnp.concatenate([kn_ref[...], pad], axis=0).astype(BF16)
        vn = jnp.concatenate([vn_ref[...], pad], axis=0).astype(BF16)
        s_idx = lax.broadcasted_iota(jnp.int32, (nl, tw), 1)
        t_idx = lax.broadcasted_iota(jnp.int32, (nl, tw), 0) % tq
        w = weights(_dot_nt(qm, kn) + bc_ref[...] * LOG2E, s_idx < t_idx)
        acc_scr[...] = _dot(w, vn)

    kt = jnp.concatenate([r[...].astype(BF16) for r in kc_refs], axis=1)
    w = weights(_dot(qm_scr[...], kt) + bc_ref[...] * LOG2E, None)
    vt = jnp.concatenate([r[...].astype(BF16) for r in vc_refs], axis=1)
    acc_scr[...] += _dot_nt(w, vt)

    @pl.when(g == pl.num_programs(1) - 1)
    def _():
        a = jnp.where(same_head, acc_scr[...], 0.0)
        out = a[0:tq, :]
        for h in range(1, n_heads):
            out = out + a[h * tq:(h + 1) * tq, :]
        o_ref[...] = out.astype(o_ref.dtype)


def _sb_paged(q, k_new, v_new, cache_kt, cache_vt, page_table, bias_col, tri, *, layer, n_heads, d_head, tq,
              n_grp):
    n, d = q.shape
    bs, n_pages = page_table.shape
    page = cache_kt.shape[3]
    nl = n_heads * tq
    assert (n_grp * page) % tri.shape[0] == 0 and n_pages % n_grp == 0
    tok = lambda b, g, pt: (b, 0)
    const = lambda b, g, pt: (0, 0)

    def cache(j):
        return pl.BlockSpec((None, None, d, page),
                            lambda b, g, pt: (layer, pt[b, n_pages - n_grp * (g + 1) + j], 0, 0))

    return pl.pallas_call(
        functools.partial(_sb_paged_kernel, n_heads=n_heads, d_head=d_head, tq=tq, n_grp=n_grp),
        grid_spec=pltpu.PrefetchScalarGridSpec(
            num_scalar_prefetch=1,
            grid=(bs, n_pages // n_grp),
            in_specs=([pl.BlockSpec((tq, d), tok)] * 3
                      + [cache(j) for j in range(n_grp)] * 2
                      + [pl.BlockSpec((nl, 1), const), pl.BlockSpec(tri.shape, const)]),
            out_specs=pl.BlockSpec((tq, d), tok),
            scratch_shapes=[pltpu.VMEM((nl, d), F32), pltpu.VMEM((nl, 1), F32), pltpu.VMEM((nl, d), BF16)],
        ),
        out_shape=jax.ShapeDtypeStruct((n, d), F32),
        compiler_params=pltpu.CompilerParams(
            dimension_semantics=("arbitrary", "arbitrary"), vmem_limit_bytes=VMEM_LIMIT_BYTES),
        name="sb_paged",
    )(page_table, q, k_new, v_new, *([cache_kt] * n_grp), *([cache_vt] * n_grp), bias_col, tri)


def _gla_kernel(qk_ref, v_ref, alr_ref, wa_ref, ba_ref, s0_ref, g_ref, cm_ref, o_ref, sfin_ref, s_scr, st_scr,
                *, nseq, tb, chunk, sub, n_heads, dk, dv):
    t_blk = pl.program_id(1)
    kw = n_heads * dk
    n_sub = chunk // sub

    @pl.when(t_blk == 0)
    def _():
        s_scr[...] = s0_ref[...]

    ti = lax.broadcasted_iota(jnp.int32, (chunk, chunk), 0)
    si = lax.broadcasted_iota(jnp.int32, (chunk, chunk), 1)
    lag = ti // sub - si // sub
    diag_mask = (lag == 0) & (si <= ti)
    eye = (lax.broadcasted_iota(jnp.int32, (dk, dk), 0) == lax.broadcasted_iota(jnp.int32, (dk, dk), 1))

    a_hi, a_lo = _split_bf16(alr_ref[...])
    w_hi, w_lo = _split_bf16(wa_ref[...])
    pre = _dot(a_hi, w_hi) + _dot(a_lo, w_hi) + _dot(a_hi, w_lo) + ba_ref[...]
    log_a = (jnp.minimum(pre, 0.0) - jnp.log(1.0 + jnp.exp(-jnp.abs(pre)))) * (1.0 / GLA_GATE_TAU)
    for ci in range(nseq * tb // chunk):
        l_hi, l_lo = _split_bf16(log_a[ci * chunk:(ci + 1) * chunk])
        st_scr[ci] = _dot(cm_ref[...], l_hi) + _dot(cm_ref[...], l_lo)

    def one_chunk(seq, r0, ci):
        rows = pl.ds(r0, chunk)
        stats = st_scr[ci]
        b_all = stats[0 * chunk:1 * chunk]
        rs_all = stats[1 * chunk:2 * chunk]
        re_all = stats[2 * chunk:3 * chunk]
        mid_all = stats[3 * chunk:4 * chunk]
        last_all = stats[4 * chunk:5 * chunk]
        qk = qk_ref[rows, :].astype(F32)
        vv = v_ref[rows, :]
        for h in range(n_heads):
            ks = slice(h * dk, (h + 1) * dk)
            q = qk[:, h * dk:(h + 1) * dk]
            k = qk[:, kw + h * dk:kw + (h + 1) * dk]
            v = vv[:, h * dv:(h + 1) * dv]
            b, rs, re, mid, last = b_all[:, ks], rs_all[:, ks], re_all[:, ks], mid_all[:, ks], last_all[:, ks]
            q_dec = q * jnp.exp(b - rs)
            k_dec = (k * jnp.exp(re - b)).astype(BF16)
            sc = jnp.where(diag_mask,
                           _dot_nt((q * jnp.exp(b - mid)).astype(BF16), (k * jnp.exp(mid - b)).astype(BF16)), 0.0)
            for d_lag in range(1, n_sub):
                ql = q_dec
                if d_lag > 1:
                    shift = sub * (d_lag - 1)
                    rs_prev = jnp.concatenate([jnp.zeros((shift, dk), F32), rs[:chunk - shift]], axis=0)
                    ql = q_dec * jnp.exp(rs - rs_prev)
                sc = sc + jnp.where(lag == d_lag, _dot_nt(ql.astype(BF16), k_dec), 0.0)
            s_old = s_scr[seq, h]
            o = _dot((q_dec * jnp.exp(rs)).astype(BF16), s_old.astype(BF16)) + _dot(sc.astype(BF16), v)
            k_st = (k * jnp.exp(last - b)).astype(BF16)
            decay_col = jnp.sum(jnp.where(eye, jnp.exp(last[0:1, :]), 0.0), axis=1, keepdims=True)
            s_scr[seq, h] = s_old * decay_col + _dot_tn(k_st, v)
            on = o * lax.rsqrt(jnp.mean(o * o, axis=-1, keepdims=True) + RMS_EPS) * g_ref[...]
            o_ref[rows, h * dv:(h + 1) * dv] = on.astype(o_ref.dtype)

    n_chunks = tb // chunk
    for seq in range(nseq):
        if n_chunks == 1:
            one_chunk(seq, seq * tb, seq)
        else:
            unroll = 2 if n_chunks % 2 == 0 else 1

            def body(n, carry, seq=seq, unroll=unroll):
                for u in range(unroll):
                    ci = seq * n_chunks + n * unroll + u
                    one_chunk(seq, pl.multiple_of(ci * chunk, chunk), ci)
                return carry
            lax.fori_loop(0, n_chunks // unroll, body, 0)

    @pl.when(t_blk == pl.num_programs(1) - 1)
    def _():
        sfin_ref[...] = s_scr[...]


def _gla(p, alr, wa, ba, s0, g, cm, *, batch, seq, nseq, tb, chunk, sub, col0):
    n = batch * seq
    _, n_heads, dk, dv = s0.shape
    kw = n_heads * dk
    vw = n_heads * dv
    nt = seq // tb
    rows = nseq * tb
    tok = lambda b, t: (b * nt + t, 0)
    const = lambda b, t: (0, 0)
    return pl.pallas_call(
        functools.partial(_gla_kernel, nseq=nseq, tb=tb, chunk=chunk, sub=sub, n_heads=n_heads, dk=dk, dv=dv),
        grid=(batch // nseq, nt),
        in_specs=[
            pl.BlockSpec((rows, 2 * kw), lambda b, t: (b * nt + t, col0)),
            pl.BlockSpec((rows, vw), lambda b, t: (b * nt + t, col0 + 1)),
            pl.BlockSpec((rows, LANES), tok),
            pl.BlockSpec((LANES, kw), const),
            pl.BlockSpec((1, kw), const),
            pl.BlockSpec((nseq, n_heads, dk, dv), lambda b, t: (b, 0, 0, 0)),
            pl.BlockSpec((1, dv), const),
            pl.BlockSpec(cm.shape, const),
        ],
        out_specs=[
            pl.BlockSpec((rows, vw), tok),
            pl.BlockSpec((nseq, n_heads, dk, dv), lambda b, t: (b, 0, 0, 0)),
        ],
        out_shape=[
            jax.ShapeDtypeStruct((n, vw), BF16),
            jax.ShapeDtypeStruct(s0.shape, F32),
        ],
        scratch_shapes=[pltpu.VMEM((nseq, n_heads, dk, dv), F32),
                        pltpu.VMEM((rows // chunk, cm.shape[0], kw), F32)],
        compiler_params=pltpu.CompilerParams(
            dimension_semantics=("arbitrary", "arbitrary"), vmem_limit_bytes=VMEM_LIMIT_BYTES),
        name="gla",
    )(p, p, alr, wa, ba, s0, g, cm)


def _chunk_matrices(chunk, sub):
    t = jnp.arange(chunk)[:, None]
    j = jnp.arange(chunk)[None, :]
    s0 = (t // sub) * sub
    mats = [j <= t, j < s0, j < s0 + sub, j < s0 + sub // 2, j < chunk + 0 * t]
    return jnp.concatenate(mats, axis=0).astype(BF16)


def _mlp_kernel(x_ref, oa_ref, ob_ref, rb_ref, ga_ref, gb_ref, wo_ref, g2_ref, wu_ref, wd_ref, y_ref,
                hn_scr, acc_scr):
    f = pl.program_id(1)

    @pl.when(f == 0)
    def _():
        mixed = (ga_ref[...].astype(F32) * oa_ref[...].astype(F32)
                 + gb_ref[...].astype(F32) * (ob_ref[...].astype(F32) * rb_ref[...].astype(F32)))
        h = x_ref[...] + _dot(mixed.astype(BF16), wo_ref[...])
        acc_scr[...] = h
        hn = h * lax.rsqrt(jnp.mean(h * h, axis=-1, keepdims=True) + RMS_EPS) * g2_ref[...]
        hn_scr[...] = hn.astype(BF16)

    u = jnp.maximum(_dot(hn_scr[...], wu_ref[...]), 0.0)
    acc_scr[...] += _dot((u * u).astype(BF16), wd_ref[...])

    @pl.when(f == pl.num_programs(1) - 1)
    def _():
        y_ref[...] = acc_scr[...]


def _mlp(x, oa, ob, p, wo, g2, wu, wd, *, tm, tf, col0):
    n, d = x.shape
    dff = wu.shape[1]
    row = lambda i, f: (i, 0)
    const = lambda i, f: (0, 0)
    return pl.pallas_call(
        _mlp_kernel,
        grid=(n // tm, dff // tf),
        in_specs=[
            pl.BlockSpec((tm, d), row),
            pl.BlockSpec((tm, d), row),
            pl.BlockSpec((tm, d), row),
            pl.BlockSpec((tm, d), lambda i, f: (i, col0)),
            pl.BlockSpec((tm, d), lambda i, f: (i, col0 + 1)),
            pl.BlockSpec((tm, d), lambda i, f: (i, col0 + 2)),
            pl.BlockSpec((d, d), const),
            pl.BlockSpec((1, d), const),
            pl.BlockSpec((d, tf), lambda i, f: (0, f)),
            pl.BlockSpec((tf, d), lambda i, f: (f, 0)),
        ],
        out_specs=pl.BlockSpec((tm, d), row),
        out_shape=jax.ShapeDtypeStruct((n, d), F32),
        scratch_shapes=[pltpu.VMEM((tm, d), BF16), pltpu.VMEM((tm, d), F32)],
        compiler_params=pltpu.CompilerParams(
            dimension_semantics=("arbitrary", "arbitrary"), vmem_limit_bytes=VMEM_LIMIT_BYTES),
        name="mlp",
    )(x, oa, ob, p, p, p, wo, g2, wu, wd)


def _pick(n, target):
    t = min(n, target)
    while n % t:
        t //= 2
    return t


def kernel(x_prompt, x_sample, cache_k, cache_v, state_gla, page_table, norm1_g, w_in, q_norm_g, k_norm_g,
           sb_bias, w_alpha2, b_alpha, gla_norm_g, b_merge, w_out, norm2_g, w_up, w_down):
    bp, seq, d = x_prompt.shape
    bs, tq, _ = x_sample.shape
    depth, n_pool, page, n_heads, d_head = cache_k.shape
    _, _, g_heads, dk, dv = state_gla.shape
    rank = w_alpha2.shape[1]
    kw = g_heads * dk
    assert n_heads * d_head == d and g_heads * dv == d and 2 * kw == d and n_heads % 2 == 0
    assert 2 * d_head == LANES and rank <= LANES

    ck = jnp.transpose(cache_k, (0, 1, 3, 4, 2)).reshape(depth, n_pool, d, page)
    cv = jnp.transpose(cache_v, (0, 1, 3, 4, 2)).reshape(depth, n_pool, d, page)
    xp = x_prompt.reshape(bp * seq, d)
    xs = x_sample.reshape(bs * tq, d)

    gw = min(d, MXU_DIM)
    gi = jnp.arange(gw) // d_head
    gsum = (gi[:, None] == gi[None, :]).astype(BF16)
    ki = jnp.arange(MXU_DIM)
    tri = (ki[:, None] > ki[None, :]).astype(BF16)
    sb_tq = _pick(seq, SB_QUERY_BLOCK)
    n_grp = _pick(page_table.shape[1], SB_PAGES_PER_STEP)
    qks = jnp.concatenate([jnp.full((1, kw), dk ** -0.5, F32), jnp.ones((1, kw), F32)], axis=1)
    chunk_p, sub_p = 64, 16
    cm_p = _chunk_matrices(chunk_p, sub_p)
    cm_s = _chunk_matrices(tq, tq)
    s0_p = jnp.zeros((bp, g_heads, dk, dv), F32)

    o_qa, o_ka, o_va = 0, d, 2 * d
    o_qb, o_kb, o_vb, o_rb = 3 * d, 3 * d + kw, 3 * d + 2 * kw, 4 * d + 2 * kw
    o_alr = o_rb + d
    o_ga = o_alr + rank
    o_gb = o_ga + d

    yp, ys = xp, xs
    kp_t = vp_t = None
    outs = [[] for _ in range(4)]
    for l in range(depth):
        wl = w_in[l]
        w_main = jnp.concatenate([wl[:, :o_alr], wl[:, o_ga:]], axis=1).astype(BF16)
        w_alr = jnp.pad(wl[:, o_alr:o_ga], ((0, 0), (0, LANES - rank))).astype(BF16)
        g1 = norm1_g[l].reshape(1, d)
        qg = jnp.tile(q_norm_g[l], n_heads).reshape(1, d)
        kg = jnp.tile(k_norm_g[l], n_heads).reshape(1, d)
        wa = jnp.pad(w_alpha2[l], ((0, LANES - rank), (0, 0)))
        ba = b_alpha[l].reshape(1, kw)
        gg = gla_norm_g[l].reshape(1, dv)
        bm = b_merge[l]
        wo = w_out[l].astype(BF16)
        g2 = norm2_g[l].reshape(1, d)
        wu = w_up[l].astype(BF16)
        wd = w_down[l].astype(BF16)
        bias_col = jnp.repeat(sb_bias[l], tq).reshape(n_heads * tq, 1)

        wt_kv = w_main[:, d:3 * d].T
        kgt = kg.reshape(d, 1)

        pp, kp_t, vp_t, alr_p, kvt = _proj(yp, g1, w_main, wt_kv, w_alr, qg, kg, kgt, gsum, qks, bm,
                                           tm=_pick(seq, 1024), d_head=d_head,
                                           kv_t=(l, depth, bp, seq, kp_t, vp_t))
        oa_p = _sb_prompt(pp, kvt, sb_bias[l], tri, batch=bp, seq=seq, n_heads=n_heads, d_head=d_head, tq=sb_tq)
        ob_p, sp = _gla(pp, alr_p, wa, ba, s0_p, gg, cm_p, batch=bp, seq=seq, nseq=1, tb=_pick(seq, 512),
                        chunk=chunk_p, sub=sub_p, col0=1)
        yp = _mlp(yp, oa_p, ob_p, pp, wo, g2, wu, wd, tm=_pick(bp * seq, 512), tf=_pick(w_up.shape[2], 1024),
                  col0=3)

        ps, ksm, vsm, alr_s = _proj(ys, g1, w_main, wt_kv, w_alr, qg, kg, kgt, gsum, qks, bm,
                                    tm=_pick(bs * tq, 256), d_head=d_head)
        q_s = ps[:, :d].astype(F32)
        oa_s = _sb_paged(q_s, ksm, vsm, ck, cv, page_table, bias_col, tri, layer=l, n_heads=n_heads,
                         d_head=d_head, tq=tq, n_grp=n_grp)
        ob_s, ss = _gla(ps, alr_s, wa, ba, state_gla[l], gg, cm_s, batch=bs, seq=tq, nseq=2, tb=tq,
                        chunk=tq, sub=tq, col0=3)
        ys = _mlp(ys, oa_s, ob_s, ps, wo, g2, wu, wd, tm=_pick(bs * tq, 256), tf=_pick(w_up.shape[2], 1024),
                  col0=5)

        for lst, val in zip(outs, (sp, ksm, vsm, ss)):
            lst.append(val)

    k_prompt = jnp.transpose(kp_t.reshape(depth, bp, n_heads, d_head, seq), (0, 1, 4, 2, 3))
    v_prompt = jnp.transpose(vp_t.reshape(depth, bp, n_heads, d_head, seq), (0, 1, 4, 2, 3))
    gla_prompt = jnp.stack(outs[0])
    k_sample = jnp.stack(outs[1]).reshape(depth, bs, tq, n_heads, d_head)
    v_sample = jnp.stack(outs[2]).reshape(depth, bs, tq, n_heads, d_head)
    gla_sample = jnp.stack(outs[3])
    return (yp.reshape(bp, seq, d), ys.reshape(bs, tq, d), k_prompt, v_prompt, gla_prompt,
            k_sample, v_sample, gla_sample)
```
